```python
import jax, jax.numpy as jnp
from jax import lax
import numpy as np

D_MODEL = 1024
BATCH = 4
SEQ = 8192
DEPTH = 2
DEC_BATCH = 8
DEC_SEQ = 16
PAST_LEN = 2048

CHUNK = 64
D_RNN = 1024
N_LRU_BLOCKS = 16
LRU_BLOCK = D_RNN // N_LRU_BLOCKS
LRU_C = 8.0
CONV_W = 4
N_HEADS = 16
HEAD_DIM = 64
ATT_WIDTH = N_HEADS * HEAD_DIM
Q_BLOCK = 128
D_FF = 2816
PLE_DIM = 256
RMS_EPS = 1e-6
IN_COLS = 2 * D_RNN + 3 * ATT_WIDTH + 2 * D_MODEL
SPLIT_IDX = (D_RNN, 2 * D_RNN, 2 * D_RNN + ATT_WIDTH, 2 * D_RNN + 2 * ATT_WIDTH,
             2 * D_RNN + 3 * ATT_WIDTH, 2 * D_RNN + 3 * ATT_WIDTH + D_MODEL)

kernel_name = "hawk_stickbreaking_macaron_stream_step"


def rms_norm(x, g):
    xf = x.astype(jnp.float32)
    y = xf * lax.rsqrt(jnp.mean(xf * xf, axis=-1, keepdims=True) + RMS_EPS)
    return (y * g.astype(jnp.float32)).astype(x.dtype)


def swiglu(h, w_gate, w_up, w_down):
    return (jax.nn.silu(h @ w_gate) * (h @ w_up)) @ w_down


def causal_conv(u, buf, w, b):
    S = u.shape[1]
    full = jnp.concatenate([buf.astype(u.dtype), u], axis=1)
    y = sum((full[:, j:j + S] * w[j] for j in range(CONV_W)), b)
    return y, full[:, -(CONV_W - 1):]


def rg_lru(xc, reset, h0, w_a, b_a, w_x, b_x, lam):
    B, S, _ = xc.shape
    xb = xc.reshape(B, S, N_LRU_BLOCKS, LRU_BLOCK)
    r = jax.nn.sigmoid(jnp.einsum("bsnc,ncd->bsnd", xb, w_a) + b_a).reshape(B, S, D_RNN)
    i = jax.nn.sigmoid(jnp.einsum("bsnc,ncd->bsnd", xb, w_x) + b_x).reshape(B, S, D_RNN)
    log_a = -LRU_C * jax.nn.softplus(-lam.astype(jnp.float32)) * r.astype(jnp.float32)
    rs = reset[None, :, None]
    a = jnp.where(rs, 0.0, jnp.exp(log_a))
    mult = jnp.where(rs, 1.0, jnp.sqrt(-jnp.expm1(2.0 * log_a)))
    bterm = mult * (i * xc).astype(jnp.float32)

    def combine(lhs, rhs):
        a1, b1 = lhs
        a2, b2 = rhs
        return a1 * a2, a2 * b1 + b2

    a_cum, b_cum = lax.associative_scan(combine, (a, bterm), axis=1)
    h = b_cum + a_cum * h0.astype(jnp.float32)[:, None, :]
    return h, h[:, -1]


def stick_breaking(q, k, v, q_pos, k_pos):
    B, Sq, H, dh = q.shape
    blk = min(Q_BLOCK, Sq)
    nb = Sq // blk
    qb = q.reshape(B, nb, blk, H, dh).transpose(1, 0, 2, 3, 4)
    pb = q_pos.reshape(nb, blk)
    scale = dh ** -0.5

    def one_block(args):
        qi, pi = args
        z = jnp.einsum("bqhd,bkhd->bhqk", qi, k).astype(jnp.float32) * scale
        mask = k_pos[None, :] < pi[:, None]
        log_1m = jnp.where(mask, jax.nn.log_sigmoid(-z), 0.0)
        tail = lax.cumsum(log_1m, axis=3, reverse=True) - log_1m
        w = jnp.where(mask, jnp.exp(jax.nn.log_sigmoid(z) + tail), 0.0)
        return jnp.einsum("bhqk,bkhd->bqhd", w.astype(v.dtype), v)

    o = lax.map(one_block, (qb, pb))
    return o.transpose(1, 0, 2, 3, 4).reshape(B, Sq, H * dh)


def layer(x, p, lw, k_past, v_past, h0, conv_buf, q_pos, k_pos):
    (n_ffn1, f1_g, f1_u, f1_d, n_mix, w_in, conv_w, conv_b, wa, ba, wx, bx, lam,
     w_br, w_ba, w_out, n_ffn2, f2_g, f2_u, f2_d, n_ple, ple_g, ple_p) = lw
    B, S, _ = x.shape
    x = x + 0.5 * swiglu(rms_norm(x, n_ffn1), f1_g, f1_u, f1_d)
    hn = rms_norm(x, n_mix)
    u_x, u_g, q, k, v, g_r, g_a = jnp.split(hn @ w_in, SPLIT_IDX, axis=-1)
    xc, new_conv = causal_conv(u_x, conv_buf, conv_w, conv_b)
    hseq, h_last = rg_lru(xc, q_pos == 0, h0, wa, ba, wx, bx, lam)
    y_rnn = hseq.astype(x.dtype) * jax.nn.gelu(u_g)
    q = q.reshape(B, S, N_HEADS, HEAD_DIM)
    k = k.reshape(B, S, N_HEADS, HEAD_DIM)
    v = v.reshape(B, S, N_HEADS, HEAD_DIM)
    k_all = jnp.concatenate([k_past.astype(k.dtype), k], axis=1)
    v_all = jnp.concatenate([v_past.astype(v.dtype), v], axis=1)
    o = stick_breaking(q, k_all, v_all, q_pos, k_pos)
    merged = jax.nn.sigmoid(g_r) * (y_rnn @ w_br) + jax.nn.sigmoid(g_a) * (o @ w_ba)
    x = x + merged @ w_out
    x = x + 0.5 * swiglu(rms_norm(x, n_ffn2), f2_g, f2_u, f2_d)
    x = x + jax.nn.sigmoid(rms_norm(x, n_ple) @ ple_g) * (p.astype(x.dtype) @ ple_p)
    return x, k, v, h_last, new_conv


def setup_inputs(seed: int = 0) -> dict:
    key = jax.random.key(seed)
    ks = iter(jax.random.split(key, 48))

    def nrm(shape, scale):
        return scale * jax.random.normal(next(ks), shape, jnp.float32)

    def gain(shape):
        return 1.0 + nrm(shape, 0.02)

    u = jax.random.uniform(next(ks), (DEPTH, D_RNN), jnp.float32, minval=0.9, maxval=0.999)
    s = u ** (1.0 / LRU_C)
    lru_lambda = jnp.log(s) - jnp.log1p(-s)
    return {
        "x_prompt": nrm((BATCH, SEQ, D_MODEL), 1.0),
        "x_sample": nrm((DEC_BATCH, DEC_SEQ, D_MODEL), 1.0),
        "p_prompt": nrm((DEPTH, BATCH, SEQ, PLE_DIM), 1.0),
        "p_sample": nrm((DEPTH, DEC_BATCH, DEC_SEQ, PLE_DIM), 1.0),
        "cache_k": nrm((DEPTH, DEC_BATCH, PAST_LEN, N_HEADS, HEAD_DIM), 1.0),
        "cache_v": nrm((DEPTH, DEC_BATCH, PAST_LEN, N_HEADS, HEAD_DIM), 1.0),
        "state_h": nrm((DEPTH, DEC_BATCH, D_RNN), 0.5),
        "state_conv": nrm((DEPTH, DEC_BATCH, CONV_W - 1, D_RNN), 1.0),
        "norm_ffn1": gain((DEPTH, D_MODEL)),
        "ffn1_w_gate": nrm((DEPTH, D_MODEL, D_FF), D_MODEL ** -0.5),
        "ffn1_w_up": nrm((DEPTH, D_MODEL, D_FF), D_MODEL ** -0.5),
        "ffn1_w_down": nrm((DEPTH, D_FF, D_MODEL), D_FF ** -0.5),
        "norm_mix": gain((DEPTH, D_MODEL)),
        "w_in": nrm((DEPTH, D_MODEL, IN_COLS), D_MODEL ** -0.5),
        "conv_w": nrm((DEPTH, CONV_W, D_RNN), CONV_W ** -0.5),
        "conv_b": nrm((DEPTH, D_RNN), 0.01),
        "lru_w_a": nrm((DEPTH, N_LRU_BLOCKS, LRU_BLOCK, LRU_BLOCK), LRU_BLOCK ** -0.5),
        "lru_b_a": nrm((DEPTH, N_LRU_BLOCKS, LRU_BLOCK), 0.01),
        "lru_w_x": nrm((DEPTH, N_LRU_BLOCKS, LRU_BLOCK, LRU_BLOCK), LRU_BLOCK ** -0.5),
        "lru_b_x": nrm((DEPTH, N_LRU_BLOCKS, LRU_BLOCK), 0.01),
        "lru_lambda": lru_lambda,
        "w_branch_rnn": nrm((DEPTH, D_RNN, D_MODEL), D_RNN ** -0.5),
        "w_branch_attn": nrm((DEPTH, ATT_WIDTH, D_MODEL), ATT_WIDTH ** -0.5),
        "w_out": nrm((DEPTH, D_MODEL, D_MODEL), D_MODEL ** -0.5),
        "norm_ffn2": gain((DEPTH, D_MODEL)),
        "ffn2_w_gate": nrm((DEPTH, D_MODEL, D_FF), D_MODEL ** -0.5),
        "ffn2_w_up": nrm((DEPTH, D_MODEL, D_FF), D_MODEL ** -0.5),
        "ffn2_w_down": nrm((DEPTH, D_FF, D_MODEL), D_FF ** -0.5),
        "norm_ple": gain((DEPTH, D_MODEL)),
        "ple_w_gate": nrm((DEPTH, D_MODEL, D_MODEL), D_MODEL ** -0.5),
        "ple_w_proj": nrm((DEPTH, PLE_DIM, D_MODEL), PLE_DIM ** -0.5),
        "final_norm": gain((D_MODEL,)),
    }


def reference(x_prompt, x_sample, p_prompt, p_sample, cache_k, cache_v, state_h, state_conv,
              norm_ffn1, ffn1_w_gate, ffn1_w_up, ffn1_w_down, norm_mix, w_in, conv_w, conv_b,
              lru_w_a, lru_b_a, lru_w_x, lru_b_x, lru_lambda, w_branch_rnn, w_branch_attn, w_out,
              norm_ffn2, ffn2_w_gate, ffn2_w_up, ffn2_w_down, norm_ple, ple_w_gate, ple_w_proj,
              final_norm):
    B, S = x_prompt.shape[0], x_prompt.shape[1]
    Sd = x_sample.shape[1]
    P = cache_k.shape[2]
    pos_p = jnp.arange(S, dtype=jnp.int32)
    pos_s = P + jnp.arange(Sd, dtype=jnp.int32)
    kpos_s = jnp.arange(P + Sd, dtype=jnp.int32)
    xp, xs = x_prompt, x_sample
    kp_l, vp_l, hp_l, cp_l, ks_l, vs_l, hs_l, cs_l = [], [], [], [], [], [], [], []
    for i in range(DEPTH):
        lw = (norm_ffn1[i], ffn1_w_gate[i], ffn1_w_up[i], ffn1_w_down[i], norm_mix[i], w_in[i],
              conv_w[i], conv_b[i], lru_w_a[i], lru_b_a[i], lru_w_x[i], lru_b_x[i], lru_lambda[i],
              w_branch_rnn[i], w_branch_attn[i], w_out[i], norm_ffn2[i], ffn2_w_gate[i],
              ffn2_w_up[i], ffn2_w_down[i], norm_ple[i], ple_w_gate[i], ple_w_proj[i])
        empty_kv = jnp.zeros((B, 0, N_HEADS, HEAD_DIM), xp.dtype)
        xp, kp, vp, hp, cp = layer(xp, p_prompt[i], lw, empty_kv, empty_kv,
                                   jnp.zeros((B, D_RNN), jnp.float32),
                                   jnp.zeros((B, CONV_W - 1, D_RNN), xp.dtype), pos_p, pos_p)
        xs, ks_, vs_, hs, cs = layer(xs, p_sample[i], lw, cache_k[i], cache_v[i], state_h[i],
                                     state_conv[i], pos_s, kpos_s)
        kp_l.append(kp); vp_l.append(vp); hp_l.append(hp); cp_l.append(cp)
        ks_l.append(ks_); vs_l.append(vs_); hs_l.append(hs); cs_l.append(cs)
    y_prompt = rms_norm(xp, final_norm)
    y_sample = rms_norm(xs, final_norm)
    return (y_prompt, y_sample,
            jnp.stack(kp_l), jnp.stack(vp_l), jnp.stack(hp_l), jnp.stack(cp_l),
            jnp.stack(ks_l), jnp.stack(vs_l), jnp.stack(hs_l), jnp.stack(cs_l))
```

```python
import functools

import jax
import jax.numpy as jnp
from jax import lax
from jax.experimental import pallas as pl
from jax.experimental.pallas import tpu as pltpu

N_HEADS = 16
HEAD_DIM = 64
N_LRU_BLOCKS = 16
LRU_C = 8.0
CONV_W = 4
RMS_EPS = 1e-6
N_IN_GROUPS = 7

LANES = 128
HEADS_PER_STEP = LANES // HEAD_DIM
VMEM_LIMIT = 56 * 1024 * 1024

F32 = jnp.float32
BF16 = jnp.bfloat16


def _rms(x, g):
    return x * lax.rsqrt(jnp.mean(x * x, axis=-1, keepdims=True) + RMS_EPS) * g


def _dot(a, b):
    return jnp.dot(a, b, preferred_element_type=F32)


def _row_spec(tm, d):
    return pl.BlockSpec((tm, d), lambda i: (i, 0))


def _resident(shape):
    return pl.BlockSpec(shape, lambda *_: (0,) * len(shape), pipeline_mode=pl.Buffered(1))


def _params(n_axes):
    return pltpu.CompilerParams(dimension_semantics=("arbitrary",) * n_axes,
                                vmem_limit_bytes=VMEM_LIMIT)


def _row_tile(rows, want):
    tm = min(rows, want)
    assert rows % tm == 0
    return tm


def _ffn_body(x_ref, g_ref, wg_ref, wu_ref, wd_ref, o_ref):
    x = x_ref[...]
    h = _rms(x, g_ref[...]).astype(BF16)
    gate = _dot(h, wg_ref[...])
    up = _dot(h, wu_ref[...])
    act = (gate * jax.nn.sigmoid(gate) * up).astype(BF16)
    o_ref[...] = x + 0.5 * _dot(act, wd_ref[...])


def _ffn(x, g, wg, wu, wd, tm):
    rows, d = x.shape
    dff = wg.shape[1]
    tm = _row_tile(rows, tm)
    return pl.pallas_call(
        _ffn_body,
        grid=(rows // tm,),
        in_specs=[_row_spec(tm, d), _resident((1, d)), _resident((d, dff)),
                  _resident((d, dff)), _resident((dff, d))],
        out_specs=_row_spec(tm, d),
        out_shape=jax.ShapeDtypeStruct((rows, d), F32),
        compiler_params=_params(1),
        name="ffn",
    )(x, g, wg, wu, wd)


def _inproj_body(x_ref, g_ref, w_ref, ux_ref, gg_ref, q_ref, k_ref, v_ref, kb_ref, vb_ref,
                 sr_ref, sa_ref):
    d = x_ref.shape[1]
    h = _rms(x_ref[...], g_ref[...]).astype(BF16)

    def col(i):
        return _dot(h, w_ref[:, i * d:(i + 1) * d])

    ux_ref[...] = col(0)
    gg_ref[...] = jax.nn.gelu(col(1)).astype(BF16)
    q_ref[...] = (col(2) * (HEAD_DIM ** -0.5)).astype(BF16)
    k = col(3)
    k_ref[...] = k
    kb_ref[...] = k.astype(BF16)
    v = col(4)
    v_ref[...] = v
    vb_ref[...] = v.astype(BF16)
    sr_ref[...] = jax.nn.sigmoid(col(5)).astype(BF16)
    sa_ref[...] = jax.nn.sigmoid(col(6)).astype(BF16)


def _inproj(x, g, w_in, tm):
    rows, d = x.shape
    tm = _row_tile(rows, tm)
    dts = (F32, BF16, BF16, F32, F32, BF16, BF16, BF16, BF16)
    return pl.pallas_call(
        _inproj_body,
        grid=(rows // tm,),
        in_specs=[_row_spec(tm, d), _resident((1, d)), _resident((d, N_IN_GROUPS * d))],
        out_specs=[_row_spec(tm, d)] * len(dts),
        out_shape=[jax.ShapeDtypeStruct((rows, d), t) for t in dts],
        compiler_params=_params(1),
        name="inproj",
    )(x, g, w_in)


def _rnn_body(pos0, ux_ref, gg_ref, cw_ref, cb_ref, wa_ref, ba_ref, wx_ref, bx_ref, lam_ref,
              h0_ref, cbuf_ref, y_ref, hl_ref, nc_ref, ubuf, a_s, b_s, h_s, hc):
    t_idx = pl.program_id(1)
    tt = ux_ref.shape[1]
    pad = ubuf.shape[0] - tt
    hist = CONV_W - 1

    @pl.when(t_idx == 0)
    def _():
        ubuf[pad - hist:pad, :] = cbuf_ref[0]
        hc[...] = h0_ref[0]

    @pl.when(t_idx != 0)
    def _():
        ubuf[0:pad, :] = ubuf[tt:tt + pad, :]

    u = ux_ref[0]
    ubuf[pad:pad + tt, :] = u
    xc = cb_ref[...]
    for j in range(CONV_W):
        lo = pad - hist + j
        xc = xc + ubuf[lo:lo + tt, :] * cw_ref[j:j + 1, :]

    xb = xc.astype(BF16)
    r = jax.nn.sigmoid(_dot(xb, wa_ref[...]) + ba_ref[...])
    gate_i = jax.nn.sigmoid(_dot(xb, wx_ref[...]) + bx_ref[...])
    lam = lam_ref[...]
    softplus_neg_lam = jnp.maximum(-lam, 0.0) + jnp.log1p(jnp.exp(-jnp.abs(lam)))
    log_a = (-LRU_C * softplus_neg_lam) * r
    pos = pos0 + t_idx * tt + lax.broadcasted_iota(jnp.int32, (tt, 1), 0)
    reset = pos == 0
    a_s[...] = jnp.where(reset, 0.0, jnp.exp(log_a))
    th = jnp.tanh(log_a)
    mult = jnp.where(reset, 1.0, jnp.sqrt(-2.0 * th / (1.0 - th)))
    b_s[...] = mult * (gate_i * xc)

    def step(t, h):
        h = a_s[pl.ds(t, 1), :] * h + b_s[pl.ds(t, 1), :]
        h_s[pl.ds(t, 1), :] = h
        return h

    h_end = lax.fori_loop(0, tt, step, hc[...], unroll=8)
    hc[...] = h_end
    hl_ref[0] = h_end
    nc_ref[0] = ubuf[pad + tt - hist:pad + tt, :]
    y_ref[0] = (h_s[...] * gg_ref[0].astype(F32)).astype(BF16)


def _rnn(ux, gg, cw, cb, wa, ba, wx, bx, lam, h0, cbuf, pos0, tt):
    b, s, d = ux.shape
    tt = _row_tile(s, tt)
    pad = 8
    tile = pl.BlockSpec((1, tt, d), lambda i, t: (i, t, 0))
    per_b = lambda n: pl.BlockSpec((1, n, d), lambda i, t: (i, 0, 0))
    return pl.pallas_call(
        functools.partial(_rnn_body, pos0),
        grid=(b, s // tt),
        in_specs=[tile, tile, _resident((CONV_W, d)), _resident((1, d)), _resident((d, d)),
                  _resident((1, d)), _resident((d, d)), _resident((1, d)), _resident((1, d)),
                  per_b(1), per_b(CONV_W - 1)],
        out_specs=[tile, per_b(1), per_b(CONV_W - 1)],
        out_shape=[jax.ShapeDtypeStruct((b, s, d), BF16),
                   jax.ShapeDtypeStruct((b, 1, d), F32),
                   jax.ShapeDtypeStruct((b, CONV_W - 1, d), F32)],
        scratch_shapes=[pltpu.VMEM((tt + pad, d), F32), pltpu.VMEM((tt, d), F32),
                        pltpu.VMEM((tt, d), F32), pltpu.VMEM((tt, d), F32),
                        pltpu.VMEM((1, d), F32)],
        compiler_params=_params(2),
        name="rnn",
    )(ux, gg, cw, cb, wa, ba, wx, bx, lam, h0, cbuf)


def _softplus(z):
    return jnp.maximum(z, 0.0) + jnp.log(1.0 + jnp.exp(-jnp.abs(z)))


def _suffix_sum(x, tri):
    hi = x.astype(BF16)
    lo = (x - hi.astype(F32)).astype(BF16)
    return _dot(hi, tri) + _dot(lo, tri)


def _attn_body(q_pos0, q_ref, k_ref, v_ref, o_ref):
    tq = q_ref.shape[1]
    tk = min(k_ref.shape[1], max(tq, 256))
    qi = pl.program_id(2)
    q_start = q_pos0 + qi * tq
    jd = q_start // tk

    row = lax.broadcasted_iota(jnp.int32, (tk, tk), 0)
    col = lax.broadcasted_iota(jnp.int32, (tk, tk), 1)
    tri = jnp.where(row >= col, 1.0, 0.0).astype(BF16)
    qrow = q_start + lax.broadcasted_iota(jnp.int32, (tq, tk), 0)
    kcol = jd * tk + lax.broadcasted_iota(jnp.int32, (tq, tk), 1)
    mask = kcol < qrow

    q2 = q_ref[0]
    heads = [q2[:, h * HEAD_DIM:(h + 1) * HEAD_DIM] for h in range(HEADS_PER_STEP)]

    def scores(qh, k2, h):
        kh = k2[:, h * HEAD_DIM:(h + 1) * HEAD_DIM]
        return lax.dot_general(qh, kh, (((1,), (1,)), ((), ())), preferred_element_type=F32)

    k2 = k_ref[0, pl.ds(pl.multiple_of(jd * tk, tk), tk), :]
    v2 = v_ref[0, pl.ds(pl.multiple_of(jd * tk, tk), tk), :]
    carry, acc = [], []
    for h, qh in enumerate(heads):
        z = scores(qh, k2, h)
        c = _suffix_sum(jnp.where(mask, _softplus(z), 0.0), tri)
        w = jnp.where(mask, jnp.exp(z - c), 0.0).astype(BF16)
        acc.append(_dot(w, v2[:, h * HEAD_DIM:(h + 1) * HEAD_DIM]))
        carry.append(c[:, 0:1])

    def tile_step(it, state):
        carry, acc = state
        j = jd - 1 - it
        k2 = k_ref[0, pl.ds(pl.multiple_of(j * tk, tk), tk), :]
        v2 = v_ref[0, pl.ds(pl.multiple_of(j * tk, tk), tk), :]
        new_carry, new_acc = [], []
        for h, qh in enumerate(heads):
            z = scores(qh, k2, h)
            c = _suffix_sum(_softplus(z), tri)
            w = jnp.exp(z - c - carry[h]).astype(BF16)
            new_acc.append(acc[h] + _dot(w, v2[:, h * HEAD_DIM:(h + 1) * HEAD_DIM]))
            new_carry.append(carry[h] + c[:, 0:1])
        return tuple(new_carry), tuple(new_acc)

    carry, acc = lax.fori_loop(0, jd, tile_step, (tuple(carry), tuple(acc)))
    o_ref[0] = jnp.concatenate(acc, axis=1).astype(BF16)


def _attn(q, k, v, q_pos0, tq):
    b, sq, d = q.shape
    sk = k.shape[1]
    tq = _row_tile(sq, tq)
    tk = min(sk, max(tq, 256))
    assert sk % tk == 0 and tk % tq == 0 and q_pos0 % tq == 0 and q_pos0 + sq <= sk
    n_col = d // LANES
    q_spec = pl.BlockSpec((1, tq, LANES), lambda i, c, t: (i, t, c))
    kv_spec = pl.BlockSpec((1, sk, LANES), lambda i, c, t: (i, 0, c))
    return pl.pallas_call(
        functools.partial(_attn_body, q_pos0),
        grid=(b, n_col, sq // tq),
        in_specs=[q_spec, kv_spec, kv_spec],
        out_specs=q_spec,
        out_shape=jax.ShapeDtypeStruct((b, sq, d), BF16),
        compiler_params=_params(3),
        name="attn",
    )(q, k, v)


def _merge_body(x_ref, y_ref, o_ref, sr_ref, sa_ref, wbr_ref, wba_ref, wo_ref, out_ref):
    m = (sr_ref[...].astype(F32) * _dot(y_ref[...], wbr_ref[...])
         + sa_ref[...].astype(F32) * _dot(o_ref[...], wba_ref[...]))
    out_ref[...] = x_ref[...] + _dot(m.astype(BF16), wo_ref[...])


def _merge(x, y, o, sr, sa, wbr, wba, wo, tm):
    rows, d = x.shape
    tm = _row_tile(rows, tm)
    return pl.pallas_call(
        _merge_body,
        grid=(rows // tm,),
        in_specs=[_row_spec(tm, d)] * 5 + [_resident((d, d))] * 3,
        out_specs=_row_spec(tm, d),
        out_shape=jax.ShapeDtypeStruct((rows, d), F32),
        compiler_params=_params(1),
        name="merge",
    )(x, y, o, sr, sa, wbr, wba, wo)


def _ple_body(final, x_ref, p_ref, g_ref, wg_ref, wp_ref, fg_ref, out_ref):
    x = x_ref[...]
    gate = jax.nn.sigmoid(_dot(_rms(x, g_ref[...]).astype(BF16), wg_ref[...]))
    x = x + gate * _dot(p_ref[...].astype(BF16), wp_ref[...])
    out_ref[...] = _rms(x, fg_ref[...]) if final else x


def _ple(x, p, g, wg, wp, fg, final, tm):
    rows, d = x.shape
    dp = p.shape[1]
    tm = _row_tile(rows, tm)
    return pl.pallas_call(
        functools.partial(_ple_body, final),
        grid=(rows // tm,),
        in_specs=[_row_spec(tm, d), _row_spec(tm, dp), _resident((1, d)), _resident((d, d)),
                  _resident((dp, d)), _resident((1, d))],
        out_specs=_row_spec(tm, d),
        out_shape=jax.ShapeDtypeStruct((rows, d), F32),
        compiler_params=_params(1),
        name="ple",
    )(x, p, g, wg, wp, fg)


def _block_diag(w):
    n, c, _ = w.shape
    eye = jnp.eye(n, dtype=w.dtype)
    return (eye[:, None, :, None] * w[:, :, None, :]).reshape(n * c, n * c)


def _layer(x, p, lw, k_past, v_past, h0, cbuf, pos0, final, tm, tt, tq):
    b, s, d = x.shape
    rows = b * s
    x = x.reshape(rows, d)
    x = _ffn(x, lw["n_ffn1"], lw["f1_g"], lw["f1_u"], lw["f1_d"], tm)
    ux, gg, q, k, v, kb, vb, sr, sa = _inproj(x, lw["n_mix"], lw["w_in"], tm)
    shp = (b, s, d)
    y, h_last, new_conv = _rnn(ux.reshape(shp), gg.reshape(shp), lw["conv_w"], lw["conv_b"],
                               lw["wa"], lw["ba"], lw["wx"], lw["bx"], lw["lam"], h0, cbuf,
                               pos0, tt)
    kb, vb = kb.reshape(shp), vb.reshape(shp)
    if k_past is not None:
        past = k_past.shape[1]
        tk = 256
        fill = (-(past + s)) % tk
        zeros = jnp.zeros((b, fill, d), BF16)
        kb = jnp.concatenate([k_past.reshape(b, past, d).astype(BF16), kb, zeros], axis=1)
        vb = jnp.concatenate([v_past.reshape(b, past, d).astype(BF16), vb, zeros], axis=1)
    o = _attn(q.reshape(shp), kb, vb, pos0, tq)
    x = _merge(x, y.reshape(rows, d), o.reshape(rows, d), sr, sa,
               lw["w_br"], lw["w_ba"], lw["w_out"], tm)
    x = _ffn(x, lw["n_ffn2"], lw["f2_g"], lw["f2_u"], lw["f2_d"], tm)
    x = _ple(x, p.reshape(rows, -1), lw["n_ple"], lw["ple_g"], lw["ple_p"], lw["final"], final, tm)
    return (x.reshape(shp), k.reshape(b, s, N_HEADS, HEAD_DIM), v.reshape(b, s, N_HEADS, HEAD_DIM),
            h_last.reshape(b, d), new_conv)


def kernel(x_prompt, x_sample, p_prompt, p_sample, cache_k, cache_v, state_h, state_conv, norm_ffn1, ffn1_w_gate, ffn1_w_up, ffn1_w_down, norm_mix, w_in, conv_w, conv_b, lru_w_a, lru_b_a, lru_w_x, lru_b_x, lru_lambda, w_branch_rnn, w_branch_attn, w_out, norm_ffn2, ffn2_w_gate, ffn2_w_up, ffn2_w_down, norm_ple, ple_w_gate, ple_w_proj, final_norm):
    depth = w_in.shape[0]
    b, s, d = x_prompt.shape
    bd, sd, _ = x_sample.shape
    past = cache_k.shape[2]
    row = lambda a: a.reshape(1, -1)
    xp, xs = x_prompt, x_sample
    outs = [[] for _ in range(8)]
    for i in range(depth):
        lw = dict(
            n_ffn1=row(norm_ffn1[i]), f1_g=ffn1_w_gate[i].astype(BF16),
            f1_u=ffn1_w_up[i].astype(BF16), f1_d=ffn1_w_down[i].astype(BF16),
            n_mix=row(norm_mix[i]), w_in=w_in[i].astype(BF16),
            conv_w=conv_w[i], conv_b=row(conv_b[i]),
            wa=_block_diag(lru_w_a[i]).astype(BF16), ba=row(lru_b_a[i]),
            wx=_block_diag(lru_w_x[i]).astype(BF16), bx=row(lru_b_x[i]),
            lam=row(lru_lambda[i]),
            w_br=w_branch_rnn[i].astype(BF16), w_ba=w_branch_attn[i].astype(BF16),
            w_out=w_out[i].astype(BF16),
            n_ffn2=row(norm_ffn2[i]), f2_g=ffn2_w_gate[i].astype(BF16),
            f2_u=ffn2_w_up[i].astype(BF16), f2_d=ffn2_w_down[i].astype(BF16),
            n_ple=row(norm_ple[i]), ple_g=ple_w_gate[i].astype(BF16),
            ple_p=ple_w_proj[i].astype(BF16), final=row(final_norm))
        final = i == depth - 1
        xp, kp, vp, hp, cp = _layer(
            xp, p_prompt[i], lw, None, None, jnp.zeros((b, 1, d), F32),
            jnp.zeros((b, CONV_W - 1, d), F32), 0, final, tm=256, tt=512, tq=256)
        xs, ks, vs, hs, cs = _layer(
            xs, p_sample[i], lw, cache_k[i], cache_v[i], state_h[i].reshape(bd, 1, d),
            state_conv[i], past, final, tm=bd * sd, tt=sd, tq=sd)
        for lst, val in zip(outs, (kp, vp, hp, cp, ks, vs, hs, cs)):
            lst.append(val)
    return (xp, xs) + tuple(jnp.stack(lst) for lst in outs)
```

```python
import functools

import jax
import jax.numpy as jnp
from jax import lax
from jax.experimental import pallas as pl
from jax.experimental.pallas import tpu as pltpu

N_HEADS = 16
HEAD_DIM = 64
N_LRU_BLOCKS = 16
LRU_C = 8.0
CONV_W = 4
RMS_EPS = 1e-6
N_IN_GROUPS = 7

LANES = 128
HEADS_PER_STEP = LANES // HEAD_DIM
VMEM_LIMIT = 56 * 1024 * 1024

F32 = jnp.float32
BF16 = jnp.bfloat16


def _rms(x, g):
    return x * lax.rsqrt(jnp.mean(x * x, axis=-1, keepdims=True) + RMS_EPS) * g


def _dot(a, b):
    return jnp.dot(a, b, preferred_element_type=F32)


def _row_spec(tm, d):
    return pl.BlockSpec((tm, d), lambda i: (i, 0))


def _resident(shape):
    return pl.BlockSpec(shape, lambda *_: (0,) * len(shape), pipeline_mode=pl.Buffered(1))


def _params(n_axes):
    return pltpu.CompilerParams(dimension_semantics=("arbitrary",) * n_axes,
                                vmem_limit_bytes=VMEM_LIMIT)


def _row_tile(rows, want):
    tm = min(rows, want)
    assert rows % tm == 0
    return tm


def _ffn_body(x_ref, g_ref, wg_ref, wu_ref, wd_ref, o_ref):
    x = x_ref[...]
    h = _rms(x, g_ref[...]).astype(BF16)
    gate = _dot(h, wg_ref[...])
    up = _dot(h, wu_ref[...])
    act = (gate * jax.nn.sigmoid(gate) * up).astype(BF16)
    o_ref[...] = x + 0.5 * _dot(act, wd_ref[...])


def _ffn(x, g, wg, wu, wd, tm):
    rows, d = x.shape
    dff = wg.shape[1]
    tm = _row_tile(rows, tm)
    return pl.pallas_call(
        _ffn_body,
        grid=(rows // tm,),
        in_specs=[_row_spec(tm, d), _resident((1, d)), _resident((d, dff)),
                  _resident((d, dff)), _resident((dff, d))],
        out_specs=_row_spec(tm, d),
        out_shape=jax.ShapeDtypeStruct((rows, d), F32),
        compiler_params=_params(1),
        name="ffn",
    )(x, g, wg, wu, wd)


def _inproj_body(x_ref, g_ref, w_ref, ux_ref, gg_ref, q_ref, k_ref, v_ref, kb_ref, vb_ref,
                 sr_ref, sa_ref):
    d = x_ref.shape[1]
    h = _rms(x_ref[...], g_ref[...]).astype(BF16)

    def col(i):
        return _dot(h, w_ref[:, i * d:(i + 1) * d])

    ux_ref[...] = col(0)
    gg_ref[...] = jax.nn.gelu(col(1)).astype(BF16)
    q_ref[...] = (col(2) * (HEAD_DIM ** -0.5)).astype(BF16)
    k = col(3)
    k_ref[...] = k
    kb_ref[...] = k.astype(BF16)
    v = col(4)
    v_ref[...] = v
    vb_ref[...] = v.astype(BF16)
    sr_ref[...] = jax.nn.sigmoid(col(5)).astype(BF16)
    sa_ref[...] = jax.nn.sigmoid(col(6)).astype(BF16)


def _inproj(x, g, w_in, tm):
    rows, d = x.shape
    tm = _row_tile(rows, tm)
    dts = (F32, BF16, BF16, F32, F32, BF16, BF16, BF16, BF16)
    return pl.pallas_call(
        _inproj_body,
        grid=(rows // tm,),
        in_specs=[_row_spec(tm, d), _resident((1, d)), _resident((d, N_IN_GROUPS * d))],
        out_specs=[_row_spec(tm, d)] * len(dts),
        out_shape=[jax.ShapeDtypeStruct((rows, d), t) for t in dts],
        compiler_params=_params(1),
        name="inproj",
    )(x, g, w_in)


def _rnn_body(pos0, ux_ref, gg_ref, cw_ref, cb_ref, wa_ref, ba_ref, wx_ref, bx_ref, lam_ref,
              h0_ref, cbuf_ref, y_ref, hl_ref, nc_ref, ubuf, a_s, b_s, h_s, hc):
    t_idx = pl.program_id(1)
    tt = ux_ref.shape[1]
    pad = ubuf.shape[0] - tt
    hist = CONV_W - 1

    @pl.when(t_idx == 0)
    def _():
        ubuf[pad - hist:pad, :] = cbuf_ref[0]
        hc[...] = h0_ref[0]

    @pl.when(t_idx != 0)
    def _():
        ubuf[0:pad, :] = ubuf[tt:tt + pad, :]

    u = ux_ref[0]
    ubuf[pad:pad + tt, :] = u
    xc = cb_ref[...]
    for j in range(CONV_W):
        lo = pad - hist + j
        xc = xc + ubuf[lo:lo + tt, :] * cw_ref[j:j + 1, :]

    xb = xc.astype(BF16)
    r = jax.nn.sigmoid(_dot(xb, wa_ref[...]) + ba_ref[...])
    gate_i = jax.nn.sigmoid(_dot(xb, wx_ref[...]) + bx_ref[...])
    lam = lam_ref[...]
    softplus_neg_lam = jnp.maximum(-lam, 0.0) + jnp.log1p(jnp.exp(-jnp.abs(lam)))
    log_a = (-LRU_C * softplus_neg_lam) * r
    pos = pos0 + t_idx * tt + lax.broadcasted_iota(jnp.int32, (tt, 1), 0)
    reset = pos == 0
    a_s[...] = jnp.where(reset, 0.0, jnp.exp(log_a))
    th = jnp.tanh(log_a)
    mult = jnp.where(reset, 1.0, jnp.sqrt(-2.0 * th / (1.0 - th)))
    b_s[...] = mult * (gate_i * xc)

    def step(t, h):
        h = a_s[pl.ds(t, 1), :] * h + b_s[pl.ds(t, 1), :]
        h_s[pl.ds(t, 1), :] = h
        return h

    h_end = lax.fori_loop(0, tt, step, hc[...], unroll=8)
    hc[...] = h_end
    hl_ref[0] = h_end
    nc_ref[0] = ubuf[pad + tt - hist:pad + tt, :]
    y_ref[0] = (h_s[...] * gg_ref[0].astype(F32)).astype(BF16)


def _rnn(ux, gg, cw, cb, wa, ba, wx, bx, lam, h0, cbuf, pos0, tt):
    b, s, d = ux.shape
    tt = _row_tile(s, tt)
    pad = 8
    tile = pl.BlockSpec((1, tt, d), lambda i, t: (i, t, 0))
    per_b = lambda n: pl.BlockSpec((1, n, d), lambda i, t: (i, 0, 0))
    return pl.pallas_call(
        functools.partial(_rnn_body, pos0),
        grid=(b, s // tt),
        in_specs=[tile, tile, _resident((CONV_W, d)), _resident((1, d)), _resident((d, d)),
                  _resident((1, d)), _resident((d, d)), _resident((1, d)), _resident((1, d)),
                  per_b(1), per_b(CONV_W - 1)],
        out_specs=[tile, per_b(1), per_b(CONV_W - 1)],
        out_shape=[jax.ShapeDtypeStruct((b, s, d), BF16),
                   jax.ShapeDtypeStruct((b, 1, d), F32),
                   jax.ShapeDtypeStruct((b, CONV_W - 1, d), F32)],
        scratch_shapes=[pltpu.VMEM((tt + pad, d), F32), pltpu.VMEM((tt, d), F32),
                        pltpu.VMEM((tt, d), F32), pltpu.VMEM((tt, d), F32),
                        pltpu.VMEM((1, d), F32)],
        compiler_params=_params(2),
        name="rnn",
    )(ux, gg, cw, cb, wa, ba, wx, bx, lam, h0, cbuf)


MASKED_SCORE = -1e30


def _softplus(z):
    neg_abs = lax.bitcast_convert_type(
        lax.bitcast_convert_type(z, jnp.uint32) | jnp.uint32(0x80000000), F32)
    return jnp.maximum(z, 0.0) + jnp.log(1.0 + jnp.exp(neg_abs))


def _attn_body(q_pos0, n_items, q_ref, k_ref, v_ref, o_ref, bias_s, tri_s, z_s, sp_s, c_s, w_s):
    tq, tk = bias_s.shape[1:]
    n_q = q_ref.shape[1] // tq
    nh = HEADS_PER_STEP

    row = lax.broadcasted_iota(jnp.int32, (tk, tk), 0)
    col = lax.broadcasted_iota(jnp.int32, (tk, tk), 1)
    tri_s[...] = jnp.where(row >= col, 1.0, 0.0).astype(BF16)
    qrow = q_pos0 % tk + lax.broadcasted_iota(jnp.int32, (tq, tk), 0)
    kcol = lax.broadcasted_iota(jnp.int32, (tq, tk), 1)
    bias_s[0] = jnp.zeros((tq, tk), F32)
    bias_s[1] = jnp.where(kcol < qrow, 0.0, MASKED_SCORE)
    z_s[...] = jnp.zeros(z_s.shape, F32)
    sp_s[...] = jnp.zeros(sp_s.shape, BF16)
    c_s[...] = jnp.zeros(c_s.shape, F32)
    w_s[...] = jnp.zeros(w_s.shape, BF16)

    def diag_tile(qi):
        return (q_pos0 + qi * tq) // tk

    def rows(ref, start, size):
        return ref[0, pl.ds(pl.multiple_of(start, size), size), :]

    def head(x, h):
        return x[:, h * HEAD_DIM:(h + 1) * HEAD_DIM]

    def next_item(item):
        qi, p = item
        done = p >= diag_tile(qi)
        at_end = jnp.logical_and(done, qi >= n_q - 1)
        advance = jnp.logical_and(done, jnp.logical_not(at_end))
        return (jnp.where(advance, qi + 1, qi),
                jnp.where(done, jnp.where(at_end, p, 0), p + 1))

    def step(u, state):
        items, carry, acc = state
        (qi0, p0), (_, _), (_, _), (_, p3), (qi4, p4) = items
        n = u
        s2, s2_prev = n % 2, (n + 1) % 2

        v2 = rows(v_ref, (diag_tile(qi4) - p4) * tk, tk)
        acc = tuple(jnp.where(p4 == 0, 0.0, acc[h]) + _dot(w_s[s2, h], head(v2, h))
                    for h in range(nh))
        o_ref[0, pl.ds(pl.multiple_of(qi4 * tq, tq), tq), :] = (
            jnp.concatenate(acc, axis=1).astype(BF16))

        new_carry = []
        for h in range(nh):
            c = c_s[s2_prev, h]
            base = jnp.where(p3 == 0, 0.0, carry[h])
            w_s[s2_prev, h] = jnp.exp(z_s[(n + 1) % 4, h] - c - base).astype(BF16)
            new_carry.append(base + c[:, 0:1])
        carry = tuple(new_carry)

        for h in range(nh):
            c_s[s2, h] = _dot(sp_s[s2, h], tri_s[...])

        for h in range(nh):
            sp_s[s2_prev, h] = _softplus(z_s[(n + 3) % 4, h]).astype(BF16)

        q2 = rows(q_ref, qi0 * tq, tq)
        k2 = rows(k_ref, (diag_tile(qi0) - p0) * tk, tk)
        bias = bias_s[jnp.where(p0 == 0, 1, 0)]
        for h in range(nh):
            z_s[n % 4, h] = bias + lax.dot_general(
                head(q2, h), head(k2, h), (((1,), (1,)), ((), ())), preferred_element_type=F32)

        return (next_item(items[0]),) + items[:-1], carry, acc

    zero = jnp.int32(0)
    items = ((zero, zero),) * 5
    carry = tuple(jnp.zeros((tq, 1), F32) for _ in range(nh))
    acc = tuple(jnp.zeros((tq, HEAD_DIM), F32) for _ in range(nh))

    def four_steps(_, state):
        for u in range(4):
            state = step(u, state)
        return state

    n_steps = n_items + 4
    state = lax.fori_loop(0, n_steps // 4, four_steps, (items, carry, acc))
    for u in range(n_steps % 4):
        state = step(u, state)


def _attn(q, k, v, q_pos0, tq):
    b, sq, d = q.shape
    sk = k.shape[1]
    tq = _row_tile(sq, tq)
    tk = min(sk, max(tq, 256))
    n_q = sq // tq
    assert sk % tk == 0 and q_pos0 + sq <= sk and (tq == tk or n_q == 1) and q_pos0 % tq == 0
    n_items = sum((q_pos0 + qi * tq) // tk + 1 for qi in range(n_q))
    n_col = d // LANES
    nh = HEADS_PER_STEP
    q_spec = pl.BlockSpec((1, sq, LANES), lambda i, c: (i, 0, c))
    kv_spec = pl.BlockSpec((1, sk, LANES), lambda i, c: (i, 0, c))
    return pl.pallas_call(
        functools.partial(_attn_body, q_pos0, n_items),
        grid=(b, n_col),
        in_specs=[q_spec, kv_spec, kv_spec],
        out_specs=q_spec,
        out_shape=jax.ShapeDtypeStruct((b, sq, d), BF16),
        scratch_shapes=[pltpu.VMEM((2, tq, tk), F32),
                        pltpu.VMEM((tk, tk), BF16),
                        pltpu.VMEM((4, nh, tq, tk), F32),
                        pltpu.VMEM((2, nh, tq, tk), BF16),
                        pltpu.VMEM((2, nh, tq, tk), F32),
                        pltpu.VMEM((2, nh, tq, tk), BF16)],
        compiler_params=_params(2),
        name="attn",
    )(q, k, v)


def _merge_body(x_ref, y_ref, o_ref, sr_ref, sa_ref, wbr_ref, wba_ref, wo_ref, out_ref):
    m = (sr_ref[...].astype(F32) * _dot(y_ref[...], wbr_ref[...])
         + sa_ref[...].astype(F32) * _dot(o_ref[...], wba_ref[...]))
    out_ref[...] = x_ref[...] + _dot(m.astype(BF16), wo_ref[...])


def _merge(x, y, o, sr, sa, wbr, wba, wo, tm):
    rows, d = x.shape
    tm = _row_tile(rows, tm)
    return pl.pallas_call(
        _merge_body,
        grid=(rows // tm,),
        in_specs=[_row_spec(tm, d)] * 5 + [_resident((d, d))] * 3,
        out_specs=_row_spec(tm, d),
        out_shape=jax.ShapeDtypeStruct((rows, d), F32),
        compiler_params=_params(1),
        name="merge",
    )(x, y, o, sr, sa, wbr, wba, wo)


def _ple_body(final, x_ref, p_ref, g_ref, wg_ref, wp_ref, fg_ref, out_ref):
    x = x_ref[...]
    gate = jax.nn.sigmoid(_dot(_rms(x, g_ref[...]).astype(BF16), wg_ref[...]))
    x = x + gate * _dot(p_ref[...].astype(BF16), wp_ref[...])
    out_ref[...] = _rms(x, fg_ref[...]) if final else x


def _ple(x, p, g, wg, wp, fg, final, tm):
    rows, d = x.shape
    dp = p.shape[1]
    tm = _row_tile(rows, tm)
    return pl.pallas_call(
        functools.partial(_ple_body, final),
        grid=(rows // tm,),
        in_specs=[_row_spec(tm, d), _row_spec(tm, dp), _resident((1, d)), _resident((d, d)),
                  _resident((dp, d)), _resident((1, d))],
        out_specs=_row_spec(tm, d),
        out_shape=jax.ShapeDtypeStruct((rows, d), F32),
        compiler_params=_params(1),
        name="ple",
    )(x, p, g, wg, wp, fg)


def _block_diag(w):
    n, c, _ = w.shape
    eye = jnp.eye(n, dtype=w.dtype)
    return (eye[:, None, :, None] * w[:, :, None, :]).reshape(n * c, n * c)


def _layer(x, p, lw, k_past, v_past, h0, cbuf, pos0, final, tm, tt, tq):
    b, s, d = x.shape
    rows = b * s
    x = x.reshape(rows, d)
    x = _ffn(x, lw["n_ffn1"], lw["f1_g"], lw["f1_u"], lw["f1_d"], tm)
    ux, gg, q, k, v, kb, vb, sr, sa = _inproj(x, lw["n_mix"], lw["w_in"], tm)
    shp = (b, s, d)
    y, h_last, new_conv = _rnn(ux.reshape(shp), gg.reshape(shp), lw["conv_w"], lw["conv_b"],
                               lw["wa"], lw["ba"], lw["wx"], lw["bx"], lw["lam"], h0, cbuf,
                               pos0, tt)
    kb, vb = kb.reshape(shp), vb.reshape(shp)
    if k_past is not None:
        past = k_past.shape[1]
        tk = 256
        fill = (-(past + s)) % tk
        zeros = jnp.zeros((b, fill, d), BF16)
        kb = jnp.concatenate([k_past.reshape(b, past, d).astype(BF16), kb, zeros], axis=1)
        vb = jnp.concatenate([v_past.reshape(b, past, d).astype(BF16), vb, zeros], axis=1)
    o = _attn(q.reshape(shp), kb, vb, pos0, tq)
    x = _merge(x, y.reshape(rows, d), o.reshape(rows, d), sr, sa,
               lw["w_br"], lw["w_ba"], lw["w_out"], tm)
    x = _ffn(x, lw["n_ffn2"], lw["f2_g"], lw["f2_u"], lw["f2_d"], tm)
    x = _ple(x, p.reshape(rows, -1), lw["n_ple"], lw["ple_g"], lw["ple_p"], lw["final"], final, tm)
    return (x.reshape(shp), k.reshape(b, s, N_HEADS, HEAD_DIM), v.reshape(b, s, N_HEADS, HEAD_DIM),
            h_last.reshape(b, d), new_conv)


def kernel(x_prompt, x_sample, p_prompt, p_sample, cache_k, cache_v, state_h, state_conv, norm_ffn1, ffn1_w_gate, ffn1_w_up, ffn1_w_down, norm_mix, w_in, conv_w, conv_b, lru_w_a, lru_b_a, lru_w_x, lru_b_x, lru_lambda, w_branch_rnn, w_branch_attn, w_out, norm_ffn2, ffn2_w_gate, ffn2_w_up, ffn2_w_down, norm_ple, ple_w_gate, ple_w_proj, final_norm):
    depth = w_in.shape[0]
    b, s, d = x_prompt.shape
    bd, sd, _ = x_sample.shape
    past = cache_k.shape[2]
    row = lambda a: a.reshape(1, -1)
    xp, xs = x_prompt, x_sample
    outs = [[] for _ in range(8)]
    for i in range(depth):
        lw = dict(
            n_ffn1=row(norm_ffn1[i]), f1_g=ffn1_w_gate[i].astype(BF16),
            f1_u=ffn1_w_up[i].astype(BF16), f1_d=ffn1_w_down[i].astype(BF16),
            n_mix=row(norm_mix[i]), w_in=w_in[i].astype(BF16),
            conv_w=conv_w[i], conv_b=row(conv_b[i]),
            wa=_block_diag(lru_w_a[i]).astype(BF16), ba=row(lru_b_a[i]),
            wx=_block_diag(lru_w_x[i]).astype(BF16), bx=row(lru_b_x[i]),
            lam=row(lru_lambda[i]),
            w_br=w_branch_rnn[i].astype(BF16), w_ba=w_branch_attn[i].astype(BF16),
            w_out=w_out[i].astype(BF16),
            n_ffn2=row(norm_ffn2[i]), f2_g=ffn2_w_gate[i].astype(BF16),
            f2_u=ffn2_w_up[i].astype(BF16), f2_d=ffn2_w_down[i].astype(BF16),
            n_ple=row(norm_ple[i]), ple_g=ple_w_gate[i].astype(BF16),
            ple_p=ple_w_proj[i].astype(BF16), final=row(final_norm))
        final = i == depth - 1
        xp, kp, vp, hp, cp = _layer(
            xp, p_prompt[i], lw, None, None, jnp.zeros((b, 1, d), F32),
            jnp.zeros((b, CONV_W - 1, d), F32), 0, final, tm=256, tt=512, tq=256)
        xs, ks, vs, hs, cs = _layer(
            xs, p_sample[i], lw, cache_k[i], cache_v[i], state_h[i].reshape(bd, 1, d),
            state_conv[i], past, final, tm=bd * sd, tt=sd, tq=sd)
        for lst, val in zip(outs, (kp, vp, hp, cp, ks, vs, hs, cs)):
            lst.append(val)
    return (xp, xs) + tuple(jnp.stack(lst) for lst in outs)
```

```python
import functools

import jax
import jax.numpy as jnp
from jax import lax
from jax.experimental import pallas as pl
from jax.experimental.pallas import tpu as pltpu

N_HEADS = 16
HEAD_DIM = 64
N_LRU_BLOCKS = 16
LRU_C = 8.0
CONV_W = 4
RMS_EPS = 1e-6
N_IN_GROUPS = 7

LANES = 128
HEADS_PER_STEP = LANES // HEAD_DIM
VMEM_LIMIT = 56 * 1024 * 1024

F32 = jnp.float32
BF16 = jnp.bfloat16


def _rms(x, g):
    return x * lax.rsqrt(jnp.mean(x * x, axis=-1, keepdims=True) + RMS_EPS) * g


def _dot(a, b):
    return jnp.dot(a, b, preferred_element_type=F32)


def _row_spec(tm, d):
    return pl.BlockSpec((tm, d), lambda i: (i, 0))


def _resident(shape):
    return pl.BlockSpec(shape, lambda *_: (0,) * len(shape), pipeline_mode=pl.Buffered(1))


def _params(n_axes):
    return pltpu.CompilerParams(dimension_semantics=("arbitrary",) * n_axes,
                                vmem_limit_bytes=VMEM_LIMIT)


def _row_tile(rows, want):
    tm = min(rows, want)
    assert rows % tm == 0
    return tm


def _ffn_body(x_ref, g_ref, wg_ref, wu_ref, wd_ref, o_ref):
    x = x_ref[...]
    h = _rms(x, g_ref[...]).astype(BF16)
    gate = _dot(h, wg_ref[...])
    up = _dot(h, wu_ref[...])
    act = (gate * jax.nn.sigmoid(gate) * up).astype(BF16)
    o_ref[...] = x + 0.5 * _dot(act, wd_ref[...])


def _ffn(x, g, wg, wu, wd, tm):
    rows, d = x.shape
    dff = wg.shape[1]
    tm = _row_tile(rows, tm)
    return pl.pallas_call(
        _ffn_body,
        grid=(rows // tm,),
        in_specs=[_row_spec(tm, d), _resident((1, d)), _resident((d, dff)),
                  _resident((d, dff)), _resident((dff, d))],
        out_specs=_row_spec(tm, d),
        out_shape=jax.ShapeDtypeStruct((rows, d), F32),
        compiler_params=_params(1),
        name="ffn",
    )(x, g, wg, wu, wd)


def _inproj_body(x_ref, g_ref, w_ref, k_prev, v_prev, ux_ref, gg_ref, q_ref, kb_ref, vb_ref, sr_ref,
                 sa_ref, kt_ref, vt_ref):
    del k_prev, v_prev
    d = x_ref.shape[1]
    h = _rms(x_ref[...], g_ref[...]).astype(BF16)

    def col(i):
        return _dot(h, w_ref[:, i * d:(i + 1) * d])

    def put_transposed(ref, val):
        n_seq, _, _, s = ref.shape[1:]
        vt = val.T
        for b in range(n_seq):
            ref[0, b] = vt[:, b * s:(b + 1) * s].reshape(N_HEADS, HEAD_DIM, s)

    ux_ref[...] = col(0)
    gg_ref[...] = jax.nn.gelu(col(1)).astype(BF16)
    q_ref[...] = (col(2) * (HEAD_DIM ** -0.5)).astype(BF16)
    k = col(3)
    put_transposed(kt_ref, k)
    kb_ref[...] = k.astype(BF16)
    v = col(4)
    put_transposed(vt_ref, v)
    vb_ref[...] = v.astype(BF16)
    sr_ref[...] = jax.nn.sigmoid(col(5)).astype(BF16)
    sa_ref[...] = jax.nn.sigmoid(col(6)).astype(BF16)


def _inproj(x, g, w_in, k_all, v_all, layer, depth, b, s, tm):
    rows, d = x.shape
    tm = _row_tile(rows, tm)
    assert tm % s == 0 or s % tm == 0
    n_seq, ts = max(tm // s, 1), min(tm, s)
    kv_shape = jax.ShapeDtypeStruct((depth, b, N_HEADS, HEAD_DIM, s), F32)
    kv_spec = pl.BlockSpec((1, n_seq, N_HEADS, HEAD_DIM, ts),
                           lambda i: (layer, i * tm // s, 0, 0, i % (s // ts)))
    dts = (F32, BF16, BF16, BF16, BF16, BF16, BF16)
    if k_all is None:
        k_all = v_all = jnp.zeros((1,), F32)
        aliases = {}
    else:
        aliases = {3: len(dts), 4: len(dts) + 1}
    return pl.pallas_call(
        _inproj_body,
        grid=(rows // tm,),
        in_specs=[_row_spec(tm, d), _resident((1, d)), _resident((d, N_IN_GROUPS * d)),
                  pl.BlockSpec(memory_space=pl.ANY), pl.BlockSpec(memory_space=pl.ANY)],
        out_specs=[_row_spec(tm, d)] * len(dts) + [kv_spec, kv_spec],
        out_shape=[jax.ShapeDtypeStruct((rows, d), t) for t in dts] + [kv_shape, kv_shape],
        input_output_aliases=aliases,
        compiler_params=_params(1),
        name="inproj",
    )(x, g, w_in, k_all, v_all)


def _rnn_body(pos0, ux_ref, gg_ref, cw_ref, cb_ref, wa_ref, ba_ref, wx_ref, bx_ref, lam_ref,
              h0_ref, cbuf_ref, y_ref, hl_ref, nc_ref, ubuf, a_s, b_s, h_s, hc):
    t_idx = pl.program_id(1)
    tt = ux_ref.shape[1]
    pad = ubuf.shape[0] - tt
    hist = CONV_W - 1

    @pl.when(t_idx == 0)
    def _():
        ubuf[pad - hist:pad, :] = cbuf_ref[0]
        hc[...] = h0_ref[0]

    @pl.when(t_idx != 0)
    def _():
        ubuf[0:pad, :] = ubuf[tt:tt + pad, :]

    u = ux_ref[0]
    ubuf[pad:pad + tt, :] = u
    xc = cb_ref[...]
    for j in range(CONV_W):
        lo = pad - hist + j
        xc = xc + ubuf[lo:lo + tt, :] * cw_ref[j:j + 1, :]

    xb = xc.astype(BF16)
    r = jax.nn.sigmoid(_dot(xb, wa_ref[...]) + ba_ref[...])
    gate_i = jax.nn.sigmoid(_dot(xb, wx_ref[...]) + bx_ref[...])
    lam = lam_ref[...]
    softplus_neg_lam = jnp.maximum(-lam, 0.0) + jnp.log1p(jnp.exp(-jnp.abs(lam)))
    log_a = (-LRU_C * softplus_neg_lam) * r
    pos = pos0 + t_idx * tt + lax.broadcasted_iota(jnp.int32, (tt, 1), 0)
    reset = pos == 0
    a_s[...] = jnp.where(reset, 0.0, jnp.exp(log_a))
    th = jnp.tanh(log_a)
    mult = jnp.where(reset, 1.0, jnp.sqrt(-2.0 * th / (1.0 - th)))
    b_s[...] = mult * (gate_i * xc)

    def step(t, h):
        h = a_s[pl.ds(t, 1), :] * h + b_s[pl.ds(t, 1), :]
        h_s[pl.ds(t, 1), :] = h
        return h

    h_end = lax.fori_loop(0, tt, step, hc[...], unroll=8)
    hc[...] = h_end
    hl_ref[0] = h_end
    nc_ref[0] = ubuf[pad + tt - hist:pad + tt, :]
    y_ref[0] = (h_s[...] * gg_ref[0].astype(F32)).astype(BF16)


def _rnn(ux, gg, cw, cb, wa, ba, wx, bx, lam, h0, cbuf, pos0, tt):
    b, s, d = ux.shape
    tt = _row_tile(s, tt)
    pad = 8
    tile = pl.BlockSpec((1, tt, d), lambda i, t: (i, t, 0))
    per_b = lambda n: pl.BlockSpec((1, n, d), lambda i, t: (i, 0, 0))
    return pl.pallas_call(
        functools.partial(_rnn_body, pos0),
        grid=(b, s // tt),
        in_specs=[tile, tile, _resident((CONV_W, d)), _resident((1, d)), _resident((d, d)),
                  _resident((1, d)), _resident((d, d)), _resident((1, d)), _resident((1, d)),
                  per_b(1), per_b(CONV_W - 1)],
        out_specs=[tile, per_b(1), per_b(CONV_W - 1)],
        out_shape=[jax.ShapeDtypeStruct((b, s, d), BF16),
                   jax.ShapeDtypeStruct((b, 1, d), F32),
                   jax.ShapeDtypeStruct((b, CONV_W - 1, d), F32)],
        scratch_shapes=[pltpu.VMEM((tt + pad, d), F32), pltpu.VMEM((tt, d), F32),
                        pltpu.VMEM((tt, d), F32), pltpu.VMEM((tt, d), F32),
                        pltpu.VMEM((1, d), F32)],
        compiler_params=_params(2),
        name="rnn",
    )(ux, gg, cw, cb, wa, ba, wx, bx, lam, h0, cbuf)


MASKED_SCORE = -1e30
N_STREAMS = 4
ZERO_WEIGHT_AT = 104.0


def _softplus(z):
    neg_abs = lax.bitcast_convert_type(
        lax.bitcast_convert_type(z, jnp.uint32) | jnp.uint32(0x80000000), F32)
    return jnp.maximum(z, 0.0) + jnp.log(1.0 + jnp.exp(neg_abs))


def _attn_body(q_pos0, q_ref, k_ref, v_ref, o_ref, bias_s, tri_s, z_s, sp_s, c_s, w_s, carry_s,
               acc_s):
    n_seq, sq, _ = q_ref.shape
    tq, tk = bias_s.shape[1:]
    nq_seq = sq // tq
    n_q = n_seq * nq_seq
    nh = HEADS_PER_STEP

    row = lax.broadcasted_iota(jnp.int32, (tk, tk), 0)
    col = lax.broadcasted_iota(jnp.int32, (tk, tk), 1)
    tri_s[...] = jnp.where(row >= col, 1.0, 0.0).astype(BF16)
    qrow = q_pos0 % tk + lax.broadcasted_iota(jnp.int32, (tq, tk), 0)
    kcol = lax.broadcasted_iota(jnp.int32, (tq, tk), 1)
    bias_s[0] = jnp.zeros((tq, tk), F32)
    bias_s[1] = jnp.where(kcol < qrow, 0.0, MASKED_SCORE)
    for ref in (z_s, sp_s, c_s, w_s, carry_s, acc_s, o_ref):
        ref[...] = jnp.zeros(ref.shape, ref.dtype)

    def place(qi):
        return qi // nq_seq, pl.multiple_of((qi % nq_seq) * tq, tq)

    def diag_tile(qi):
        return (q_pos0 + (qi % nq_seq) * tq) // tk

    def key_rows(ref, qi, p):
        tile = jnp.maximum(diag_tile(qi) - p, 0)
        return ref[qi // nq_seq, pl.ds(pl.multiple_of(tile * tk, tk), tk), :]

    def head(x, h):
        return x[:, h * HEAD_DIM:(h + 1) * HEAD_DIM]

    def sub_step(u, streams, next_q):
        qi, p, real, spent = streams[u]

        seq, r0 = place(qi)
        v2 = key_rows(v_ref, qi, p)
        out = []
        for h in range(nh):
            a = jnp.where(p == 0, 0.0, acc_s[u, h]) + _dot(w_s[u, h], head(v2, h))
            acc_s[u, h] = a
            out.append(a)
        o_ref[seq, pl.ds(r0, tq), :] = jnp.where(
            real == 1, jnp.concatenate(out, axis=1).astype(BF16), o_ref[seq, pl.ds(r0, tq), :])

        t = (u + 1) % N_STREAMS
        qi_t, p_t, real_t, _ = streams[t]
        low = None
        for h in range(nh):
            c = c_s[t, h]
            base = jnp.where(p_t == 0, 0.0, carry_s[t, h])
            base = jnp.where(real_t == 1, base, -MASKED_SCORE)
            w_s[t, h] = jnp.exp(z_s[t, h] - c - base).astype(BF16)
            total = base + c[:, 0:1]
            carry_s[t, h] = total
            low = jnp.min(total) if low is None else jnp.minimum(low, jnp.min(total))
        streams[t] = (qi_t, p_t, real_t, (low > ZERO_WEIGHT_AT).astype(jnp.int32))

        t = (u + 2) % N_STREAMS
        for h in range(nh):
            c_s[t, h] = _dot(sp_s[t, h], tri_s[...])

        t = (u + 3) % N_STREAMS
        for h in range(nh):
            sp_s[t, h] = _softplus(z_s[t, h]).astype(BF16)

        done = jnp.logical_or(jnp.logical_or(real == 0, p >= diag_tile(qi)), spent == 1)
        take = jnp.logical_and(done, next_q < n_q)
        qi = jnp.where(take, next_q, qi)
        p = jnp.where(done, jnp.where(take, 0, 1), p + 1)
        real = jnp.where(done, take.astype(jnp.int32), 1)
        next_q = next_q + take.astype(jnp.int32)
        streams[u] = (qi, p, real, jnp.int32(0))

        seq, r0 = place(qi)
        q2 = q_ref[seq, pl.ds(r0, tq), :]
        k2 = key_rows(k_ref, qi, p)
        bias = bias_s[jnp.where(p == 0, 1, 0)]
        for h in range(nh):
            z_s[u, h] = bias + lax.dot_general(
                head(q2, h), head(k2, h), (((1,), (1,)), ((), ())), preferred_element_type=F32)
        return streams, next_q

    def one_round(state):
        streams, next_q = list(state[0]), state[1]
        for u in range(N_STREAMS):
            streams, next_q = sub_step(u, streams, next_q)
        return tuple(streams), next_q

    def work_left(state):
        streams, next_q = state
        busy = next_q < n_q
        for _, _, real, _ in streams:
            busy = jnp.logical_or(busy, real == 1)
        return busy

    idle = (jnp.int32(0), jnp.int32(1), jnp.int32(0), jnp.int32(1))
    lax.while_loop(work_left, one_round, ((idle,) * N_STREAMS, jnp.int32(0)))


def _attn(q, k, v, q_pos0, tq, seqs_per_step):
    b, sq, d = q.shape
    sk = k.shape[1]
    tq = _row_tile(sq, tq)
    tk = min(sk, max(tq, 256))
    nb = seqs_per_step
    assert sk % tk == 0 and q_pos0 + sq <= sk and (tq == tk or sq == tq) and q_pos0 % tq == 0
    assert b % nb == 0
    n_col = d // LANES
    nh = HEADS_PER_STEP
    q_spec = pl.BlockSpec((nb, sq, LANES), lambda i, c: (i, 0, c))
    kv_spec = pl.BlockSpec((nb, sk, LANES), lambda i, c: (i, 0, c))
    ring = lambda width, dt: pltpu.VMEM((N_STREAMS, nh, tq, width), dt)
    return pl.pallas_call(
        functools.partial(_attn_body, q_pos0),
        grid=(b // nb, n_col),
        in_specs=[q_spec, kv_spec, kv_spec],
        out_specs=q_spec,
        out_shape=jax.ShapeDtypeStruct((b, sq, d), BF16),
        scratch_shapes=[pltpu.VMEM((2, tq, tk), F32), pltpu.VMEM((tk, tk), BF16),
                        ring(tk, F32), ring(tk, BF16), ring(tk, F32), ring(tk, BF16),
                        ring(1, F32), ring(HEAD_DIM, F32)],
        compiler_params=_params(2),
        name="attn",
    )(q, k, v)


def _merge_body(x_ref, y_ref, o_ref, sr_ref, sa_ref, wbr_ref, wba_ref, wo_ref, out_ref):
    m = (sr_ref[...].astype(F32) * _dot(y_ref[...], wbr_ref[...])
         + sa_ref[...].astype(F32) * _dot(o_ref[...], wba_ref[...]))
    out_ref[...] = x_ref[...] + _dot(m.astype(BF16), wo_ref[...])


def _merge(x, y, o, sr, sa, wbr, wba, wo, tm):
    rows, d = x.shape
    tm = _row_tile(rows, tm)
    return pl.pallas_call(
        _merge_body,
        grid=(rows // tm,),
        in_specs=[_row_spec(tm, d)] * 5 + [_resident((d, d))] * 3,
        out_specs=_row_spec(tm, d),
        out_shape=jax.ShapeDtypeStruct((rows, d), F32),
        compiler_params=_params(1),
        name="merge",
    )(x, y, o, sr, sa, wbr, wba, wo)


def _ple_body(final, x_ref, p_ref, g_ref, wg_ref, wp_ref, fg_ref, out_ref):
    x = x_ref[...]
    gate = jax.nn.sigmoid(_dot(_rms(x, g_ref[...]).astype(BF16), wg_ref[...]))
    x = x + gate * _dot(p_ref[...].astype(BF16), wp_ref[...])
    out_ref[...] = _rms(x, fg_ref[...]) if final else x


def _ple(x, p, g, wg, wp, fg, final, tm):
    rows, d = x.shape
    dp = p.shape[1]
    tm = _row_tile(rows, tm)
    return pl.pallas_call(
        functools.partial(_ple_body, final),
        grid=(rows // tm,),
        in_specs=[_row_spec(tm, d), _row_spec(tm, dp), _resident((1, d)), _resident((d, d)),
                  _resident((dp, d)), _resident((1, d))],
        out_specs=_row_spec(tm, d),
        out_shape=jax.ShapeDtypeStruct((rows, d), F32),
        compiler_params=_params(1),
        name="ple",
    )(x, p, g, wg, wp, fg)


def _block_diag(w):
    n, c, _ = w.shape
    eye = jnp.eye(n, dtype=w.dtype)
    return (eye[:, None, :, None] * w[:, :, None, :]).reshape(n * c, n * c)


def _layer(x, p, lw, k_past, v_past, h0, cbuf, k_all, v_all, layer, depth, pos0, final, tm, tt, tq,
           seqs):
    b, s, d = x.shape
    rows = b * s
    x = x.reshape(rows, d)
    x = _ffn(x, lw["n_ffn1"], lw["f1_g"], lw["f1_u"], lw["f1_d"], tm)
    ux, gg, q, kb, vb, sr, sa, k_all, v_all = _inproj(x, lw["n_mix"], lw["w_in"], k_all, v_all,
                                                      layer, depth, b, s, tm)
    shp = (b, s, d)
    y, h_last, new_conv = _rnn(ux.reshape(shp), gg.reshape(shp), lw["conv_w"], lw["conv_b"],
                               lw["wa"], lw["ba"], lw["wx"], lw["bx"], lw["lam"], h0, cbuf,
                               pos0, tt)
    kb, vb = kb.reshape(shp), vb.reshape(shp)
    if k_past is not None:
        past = k_past.shape[1]
        tk = 256
        fill = (-(past + s)) % tk
        zeros = jnp.zeros((b, fill, d), BF16)
        kb = jnp.concatenate([k_past.reshape(b, past, d).astype(BF16), kb, zeros], axis=1)
        vb = jnp.concatenate([v_past.reshape(b, past, d).astype(BF16), vb, zeros], axis=1)
    o = _attn(q.reshape(shp), kb, vb, pos0, tq, seqs)
    x = _merge(x, y.reshape(rows, d), o.reshape(rows, d), sr, sa,
               lw["w_br"], lw["w_ba"], lw["w_out"], tm)
    x = _ffn(x, lw["n_ffn2"], lw["f2_g"], lw["f2_u"], lw["f2_d"], tm)
    x = _ple(x, p.reshape(rows, -1), lw["n_ple"], lw["ple_g"], lw["ple_p"], lw["final"], final, tm)
    return x.reshape(shp), k_all, v_all, h_last.reshape(b, d), new_conv


def kernel(x_prompt, x_sample, p_prompt, p_sample, cache_k, cache_v, state_h, state_conv, norm_ffn1, ffn1_w_gate, ffn1_w_up, ffn1_w_down, norm_mix, w_in, conv_w, conv_b, lru_w_a, lru_b_a, lru_w_x, lru_b_x, lru_lambda, w_branch_rnn, w_branch_attn, w_out, norm_ffn2, ffn2_w_gate, ffn2_w_up, ffn2_w_down, norm_ple, ple_w_gate, ple_w_proj, final_norm):
    depth = w_in.shape[0]
    b, s, d = x_prompt.shape
    bd, sd, _ = x_sample.shape
    past = cache_k.shape[2]
    row = lambda a: a.reshape(1, -1)
    xp, xs = x_prompt, x_sample
    kp = vp = ks = vs = None
    outs = [[] for _ in range(4)]
    for i in range(depth):
        lw = dict(
            n_ffn1=row(norm_ffn1[i]), f1_g=ffn1_w_gate[i].astype(BF16),
            f1_u=ffn1_w_up[i].astype(BF16), f1_d=ffn1_w_down[i].astype(BF16),
            n_mix=row(norm_mix[i]), w_in=w_in[i].astype(BF16),
            conv_w=conv_w[i], conv_b=row(conv_b[i]),
            wa=_block_diag(lru_w_a[i]).astype(BF16), ba=row(lru_b_a[i]),
            wx=_block_diag(lru_w_x[i]).astype(BF16), bx=row(lru_b_x[i]),
            lam=row(lru_lambda[i]),
            w_br=w_branch_rnn[i].astype(BF16), w_ba=w_branch_attn[i].astype(BF16),
            w_out=w_out[i].astype(BF16),
            n_ffn2=row(norm_ffn2[i]), f2_g=ffn2_w_gate[i].astype(BF16),
            f2_u=ffn2_w_up[i].astype(BF16), f2_d=ffn2_w_down[i].astype(BF16),
            n_ple=row(norm_ple[i]), ple_g=ple_w_gate[i].astype(BF16),
            ple_p=ple_w_proj[i].astype(BF16), final=row(final_norm))
        final = i == depth - 1
        xp, kp, vp, hp, cp = _layer(
            xp, p_prompt[i], lw, None, None, jnp.zeros((b, 1, d), F32),
            jnp.zeros((b, CONV_W - 1, d), F32), kp, vp, i, depth, 0, final,
            tm=256, tt=512, tq=256, seqs=1)
        xs, ks, vs, hs, cs = _layer(
            xs, p_sample[i], lw, cache_k[i], cache_v[i], state_h[i].reshape(bd, 1, d),
            state_conv[i], ks, vs, i, depth, past, final, tm=bd * sd, tt=sd, tq=sd, seqs=bd)
        for lst, val in zip(outs, (hp, cp, hs, cs)):
            lst.append(val)
    hp, cp, hs, cs = (jnp.stack(lst) for lst in outs)
    seq_major = lambda a: jnp.transpose(a, (0, 1, 4, 2, 3))
    return (xp, xs, seq_major(kp), seq_major(vp), hp, cp, seq_major(ks), seq_major(vs), hs, cs)
```

```python
import functools

import jax
import jax.numpy as jnp
from jax import lax
from jax.experimental import pallas as pl
from jax.experimental.pallas import tpu as pltpu

N_HEADS = 16
HEAD_DIM = 64
N_LRU_BLOCKS = 16
LRU_C = 8.0
CONV_W = 4
RMS_EPS = 1e-6
N_IN_GROUPS = 7

LANES = 128
HEADS_PER_STEP = LANES // HEAD_DIM
VMEM_LIMIT = 56 * 1024 * 1024

F32 = jnp.float32
BF16 = jnp.bfloat16


def _rms(x, g):
    return x * lax.rsqrt(jnp.mean(x * x, axis=-1, keepdims=True) + RMS_EPS) * g


def _dot(a, b):
    return jnp.dot(a, b, preferred_element_type=F32)


def _row_spec(tm, d):
    return pl.BlockSpec((tm, d), lambda i: (i, 0))


def _resident(shape):
    return pl.BlockSpec(shape, lambda *_: (0,) * len(shape), pipeline_mode=pl.Buffered(1))


def _params(n_axes):
    return pltpu.CompilerParams(dimension_semantics=("arbitrary",) * n_axes,
                                vmem_limit_bytes=VMEM_LIMIT)


def _row_tile(rows, want):
    tm = min(rows, want)
    assert rows % tm == 0
    return tm


def _ffn_body(with_ple, final, x_ref, g_ref, wg_ref, wu_ref, wd_ref, *rest):
    o_ref = rest[-1]
    x = x_ref[...]
    h = _rms(x, g_ref[...]).astype(BF16)
    gate = _dot(h, wg_ref[...])
    up = _dot(h, wu_ref[...])
    act = (gate * jax.nn.sigmoid(gate) * up).astype(BF16)
    x = x + 0.5 * _dot(act, wd_ref[...])
    if with_ple:
        p_ref, pg_ref, pwg_ref, pwp_ref, fg_ref = rest[:-1]
        gate = jax.nn.sigmoid(_dot(_rms(x, pg_ref[...]).astype(BF16), pwg_ref[...]))
        x = x + gate * _dot(p_ref[0].astype(BF16), pwp_ref[...])
        if final:
            x = _rms(x, fg_ref[...])
    o_ref[...] = x


def _ffn(x, g, wg, wu, wd, tm, ple=None):
    rows, d = x.shape
    dff = wg.shape[1]
    tm = _row_tile(rows, tm)
    in_specs = [_row_spec(tm, d), _resident((1, d)), _resident((d, dff)),
                _resident((d, dff)), _resident((dff, d))]
    args = [x, g, wg, wu, wd]
    final = False
    if ple is not None:
        p_all, layer, pg, pwg, pwp, fg, final = ple
        dp = p_all.shape[-1]
        in_specs += [pl.BlockSpec((1, tm, dp), lambda i: (layer, i, 0)), _resident((1, d)),
                     _resident((d, d)), _resident((dp, d)), _resident((1, d))]
        args += [p_all, pg, pwg, pwp, fg]
    return pl.pallas_call(
        functools.partial(_ffn_body, ple is not None, final),
        grid=(rows // tm,),
        in_specs=in_specs,
        out_specs=_row_spec(tm, d),
        out_shape=jax.ShapeDtypeStruct((rows, d), F32),
        compiler_params=_params(1),
        name="ffn",
    )(*args)


def _inproj_body(x_ref, g_ref, w_ref, k_prev, v_prev, ux_ref, gg_ref, q_ref, kb_ref, vb_ref, sr_ref,
                 sa_ref, kt_ref, vt_ref):
    del k_prev, v_prev
    d = x_ref.shape[1]
    h = _rms(x_ref[...], g_ref[...]).astype(BF16)

    def col(i):
        return _dot(h, w_ref[:, i * d:(i + 1) * d])

    def put_transposed(ref, val):
        n_seq, _, _, s = ref.shape[1:]
        vt = val.T
        for b in range(n_seq):
            ref[0, b] = vt[:, b * s:(b + 1) * s].reshape(N_HEADS, HEAD_DIM, s)

    ux_ref[...] = col(0)
    gg_ref[...] = jax.nn.gelu(col(1)).astype(BF16)
    q_ref[...] = (col(2) * (HEAD_DIM ** -0.5)).astype(BF16)
    k = col(3)
    put_transposed(kt_ref, k)
    kb_ref[...] = k.astype(BF16)
    v = col(4)
    put_transposed(vt_ref, v)
    vb_ref[...] = v.astype(BF16)
    sr_ref[...] = jax.nn.sigmoid(col(5)).astype(BF16)
    sa_ref[...] = jax.nn.sigmoid(col(6)).astype(BF16)


def _inproj(x, g, w_in, k_all, v_all, layer, depth, b, s, tm):
    rows, d = x.shape
    tm = _row_tile(rows, tm)
    assert tm % s == 0 or s % tm == 0
    n_seq, ts = max(tm // s, 1), min(tm, s)
    kv_shape = jax.ShapeDtypeStruct((depth, b, N_HEADS, HEAD_DIM, s), F32)
    kv_spec = pl.BlockSpec((1, n_seq, N_HEADS, HEAD_DIM, ts),
                           lambda i: (layer, i * tm // s, 0, 0, i % (s // ts)))
    dts = (F32, BF16, BF16, BF16, BF16, BF16, BF16)
    if k_all is None:
        k_all = v_all = jnp.zeros((1,), F32)
        aliases = {}
    else:
        aliases = {3: len(dts), 4: len(dts) + 1}
    return pl.pallas_call(
        _inproj_body,
        grid=(rows // tm,),
        in_specs=[_row_spec(tm, d), _resident((1, d)), _resident((d, N_IN_GROUPS * d)),
                  pl.BlockSpec(memory_space=pl.ANY), pl.BlockSpec(memory_space=pl.ANY)],
        out_specs=[_row_spec(tm, d)] * len(dts) + [kv_spec, kv_spec],
        out_shape=[jax.ShapeDtypeStruct((rows, d), t) for t in dts] + [kv_shape, kv_shape],
        input_output_aliases=aliases,
        compiler_params=_params(1),
        name="inproj",
    )(x, g, w_in, k_all, v_all)


def _rnn_body(pos0, ux_ref, gg_ref, cw_ref, cb_ref, wa_ref, ba_ref, wx_ref, bx_ref, lam_ref,
              h0_ref, cbuf_ref, y_ref, hl_ref, nc_ref, ubuf, a_s, b_s, h_s, hc):
    t_idx = pl.program_id(1)
    tt = ux_ref.shape[1]
    pad = ubuf.shape[0] - tt
    hist = CONV_W - 1

    @pl.when(t_idx == 0)
    def _():
        ubuf[pad - hist:pad, :] = cbuf_ref[0]
        hc[...] = h0_ref[0]

    @pl.when(t_idx != 0)
    def _():
        ubuf[0:pad, :] = ubuf[tt:tt + pad, :]

    u = ux_ref[0]
    ubuf[pad:pad + tt, :] = u
    first = lax.broadcasted_iota(jnp.int32, (8, 1), 0) == 0

    def shift_down(x, row_before):
        rolled = pltpu.roll(x, 1, 0)
        return jnp.concatenate([jnp.where(first, row_before, rolled[0:8]), rolled[8:]], axis=0)

    w = [cw_ref[j:j + 1, :] for j in range(CONV_W)]
    um1, um2, um3 = (ubuf[pad - j:pad - j + 1, :] for j in (1, 2, 3))
    acc = shift_down(u * w[0], um1 * w[0])
    acc = shift_down(u * w[1] + acc, um1 * w[1] + um2 * w[0])
    acc = shift_down(u * w[2] + acc, um1 * w[2] + (um2 * w[1] + um3 * w[0]))
    xc = cb_ref[...] + (u * w[3] + acc)

    xb = xc.astype(BF16)
    r = jax.nn.sigmoid(_dot(xb, wa_ref[...]) + ba_ref[...])
    gate_i = jax.nn.sigmoid(_dot(xb, wx_ref[...]) + bx_ref[...])
    lam = lam_ref[...]
    softplus_neg_lam = jnp.maximum(-lam, 0.0) + jnp.log1p(jnp.exp(-jnp.abs(lam)))
    log_a = (-LRU_C * softplus_neg_lam) * r
    a_s[...] = jnp.exp(log_a)
    th = jnp.tanh(log_a)
    sq = -2.0 * th / (1.0 - th)
    gated = gate_i * xc
    b_s[...] = jnp.where(sq > 0.0, sq * lax.rsqrt(sq), 0.0) * gated
    if pos0 == 0:
        @pl.when(t_idx == 0)
        def _():
            a_s[0:1, :] = jnp.zeros((1, a_s.shape[1]), F32)
            b_s[0:1, :] = gated[0:1, :]

    def step(t, h):
        h = a_s[pl.ds(t, 1), :] * h + b_s[pl.ds(t, 1), :]
        h_s[pl.ds(t, 1), :] = h
        return h

    h_end = lax.fori_loop(0, tt, step, hc[...], unroll=8)
    hc[...] = h_end
    hl_ref[0] = h_end
    nc_ref[0] = ubuf[pad + tt - hist:pad + tt, :]
    y_ref[0] = (h_s[...] * gg_ref[0].astype(F32)).astype(BF16)


def _rnn(ux, gg, cw, cb, wa, ba, wx, bx, lam, h0, cbuf, pos0, tt):
    b, s, d = ux.shape
    tt = _row_tile(s, tt)
    pad = 8
    tile = pl.BlockSpec((1, tt, d), lambda i, t: (i, t, 0))
    per_b = lambda n: pl.BlockSpec((1, n, d), lambda i, t: (i, 0, 0))
    return pl.pallas_call(
        functools.partial(_rnn_body, pos0),
        grid=(b, s // tt),
        in_specs=[tile, tile, _resident((CONV_W, d)), _resident((1, d)), _resident((d, d)),
                  _resident((1, d)), _resident((d, d)), _resident((1, d)), _resident((1, d)),
                  per_b(1), per_b(CONV_W - 1)],
        out_specs=[tile, per_b(1), per_b(CONV_W - 1)],
        out_shape=[jax.ShapeDtypeStruct((b, s, d), BF16),
                   jax.ShapeDtypeStruct((b, 1, d), F32),
                   jax.ShapeDtypeStruct((b, CONV_W - 1, d), F32)],
        scratch_shapes=[pltpu.VMEM((tt + pad, d), F32), pltpu.VMEM((tt, d), F32),
                        pltpu.VMEM((tt, d), F32), pltpu.VMEM((tt, d), F32),
                        pltpu.VMEM((1, d), F32)],
        compiler_params=_params(2),
        name="rnn",
    )(ux, gg, cw, cb, wa, ba, wx, bx, lam, h0, cbuf)


MASKED_SCORE = -1e30
N_STREAMS = 4
ZERO_WEIGHT_AT = 104.0


def _softplus(z):
    neg_abs = lax.bitcast_convert_type(
        lax.bitcast_convert_type(z, jnp.uint32) | jnp.uint32(0x80000000), F32)
    return jnp.maximum(z, 0.0) + jnp.log(1.0 + jnp.exp(neg_abs))


def _attn_body(q_pos0, q_ref, k_ref, v_ref, o_ref, bias_s, tri_s, z_s, sp_s, c_s, w_s, carry_s,
               acc_s):
    n_seq, sq, _ = q_ref.shape
    tq, tk = bias_s.shape[1:]
    nq_seq = sq // tq
    n_q = n_seq * nq_seq
    nh = HEADS_PER_STEP

    row = lax.broadcasted_iota(jnp.int32, (tk, tk), 0)
    col = lax.broadcasted_iota(jnp.int32, (tk, tk), 1)
    tri_s[...] = jnp.where(row >= col, 1.0, 0.0).astype(BF16)
    qrow = q_pos0 % tk + lax.broadcasted_iota(jnp.int32, (tq, tk), 0)
    kcol = lax.broadcasted_iota(jnp.int32, (tq, tk), 1)
    bias_s[0] = jnp.zeros((tq, tk), F32)
    bias_s[1] = jnp.where(kcol < qrow, 0.0, MASKED_SCORE)
    for ref in (z_s, sp_s, c_s, w_s, carry_s, acc_s, o_ref):
        ref[...] = jnp.zeros(ref.shape, ref.dtype)

    def place(qi):
        return qi // nq_seq, pl.multiple_of((qi % nq_seq) * tq, tq)

    def diag_tile(qi):
        return (q_pos0 + (qi % nq_seq) * tq) // tk

    def key_rows(ref, qi, p):
        tile = jnp.maximum(diag_tile(qi) - p, 0)
        return ref[qi // nq_seq, pl.ds(pl.multiple_of(tile * tk, tk), tk), :]

    def head(x, h):
        return x[:, h * HEAD_DIM:(h + 1) * HEAD_DIM]

    def sub_step(u, streams, next_q):
        qi, p, real, spent = streams[u]

        seq, r0 = place(qi)
        v2 = key_rows(v_ref, qi, p)
        out = []
        for h in range(nh):
            a = jnp.where(p == 0, 0.0, acc_s[u, h]) + _dot(w_s[u, h], head(v2, h))
            acc_s[u, h] = a
            out.append(a)
        o_ref[seq, pl.ds(r0, tq), :] = jnp.where(
            real == 1, jnp.concatenate(out, axis=1).astype(BF16), o_ref[seq, pl.ds(r0, tq), :])

        t = (u + 1) % N_STREAMS
        qi_t, p_t, real_t, _ = streams[t]
        low = None
        for h in range(nh):
            c = c_s[t, h]
            base = jnp.where(p_t == 0, 0.0, carry_s[t, h])
            base = jnp.where(real_t == 1, base, -MASKED_SCORE)
            w_s[t, h] = jnp.exp(z_s[t, h] - c - base).astype(BF16)
            total = base + c[:, 0:1]
            carry_s[t, h] = total
            low = jnp.min(total) if low is None else jnp.minimum(low, jnp.min(total))
        streams[t] = (qi_t, p_t, real_t, (low > ZERO_WEIGHT_AT).astype(jnp.int32))

        t = (u + 2) % N_STREAMS
        for h in range(nh):
            c_s[t, h] = _dot(sp_s[t, h], tri_s[...])

        t = (u + 3) % N_STREAMS
        for h in range(nh):
            sp_s[t, h] = _softplus(z_s[t, h]).astype(BF16)

        done = jnp.logical_or(jnp.logical_or(real == 0, p >= diag_tile(qi)), spent == 1)
        take = jnp.logical_and(done, next_q < n_q)
        qi = jnp.where(take, next_q, qi)
        p = jnp.where(done, jnp.where(take, 0, 1), p + 1)
        real = jnp.where(done, take.astype(jnp.int32), 1)
        next_q = next_q + take.astype(jnp.int32)
        streams[u] = (qi, p, real, jnp.int32(0))

        seq, r0 = place(qi)
        q2 = q_ref[seq, pl.ds(r0, tq), :]
        k2 = key_rows(k_ref, qi, p)
        bias = bias_s[jnp.where(p == 0, 1, 0)]
        for h in range(nh):
            z_s[u, h] = bias + lax.dot_general(
                head(q2, h), head(k2, h), (((1,), (1,)), ((), ())), preferred_element_type=F32)
        return streams, next_q

    def one_round(state):
        streams, next_q = list(state[0]), state[1]
        for u in range(N_STREAMS):
            streams, next_q = sub_step(u, streams, next_q)
        return tuple(streams), next_q

    def work_left(state):
        streams, next_q = state
        busy = next_q < n_q
        for _, _, real, _ in streams:
            busy = jnp.logical_or(busy, real == 1)
        return busy

    idle = (jnp.int32(0), jnp.int32(1), jnp.int32(0), jnp.int32(1))
    lax.while_loop(work_left, one_round, ((idle,) * N_STREAMS, jnp.int32(0)))


def _attn(q, k, v, q_pos0, tq, seqs_per_step):
    b, sq, d = q.shape
    sk = k.shape[1]
    tq = _row_tile(sq, tq)
    tk = min(sk, max(tq, 256))
    nb = seqs_per_step
    assert sk % tk == 0 and q_pos0 + sq <= sk and (tq == tk or sq == tq) and q_pos0 % tq == 0
    assert b % nb == 0
    n_col = d // LANES
    nh = HEADS_PER_STEP
    q_spec = pl.BlockSpec((nb, sq, LANES), lambda i, c: (i, 0, c))
    kv_spec = pl.BlockSpec((nb, sk, LANES), lambda i, c: (i, 0, c))
    ring = lambda width, dt: pltpu.VMEM((N_STREAMS, nh, tq, width), dt)
    return pl.pallas_call(
        functools.partial(_attn_body, q_pos0),
        grid=(b // nb, n_col),
        in_specs=[q_spec, kv_spec, kv_spec],
        out_specs=q_spec,
        out_shape=jax.ShapeDtypeStruct((b, sq, d), BF16),
        scratch_shapes=[pltpu.VMEM((2, tq, tk), F32), pltpu.VMEM((tk, tk), BF16),
                        ring(tk, F32), ring(tk, BF16), ring(tk, F32), ring(tk, BF16),
                        ring(1, F32), ring(HEAD_DIM, F32)],
        compiler_params=_params(2),
        name="attn",
    )(q, k, v)


def _merge_body(x_ref, y_ref, o_ref, sr_ref, sa_ref, wbr_ref, wba_ref, wo_ref, out_ref):
    m = (sr_ref[...].astype(F32) * _dot(y_ref[...], wbr_ref[...])
         + sa_ref[...].astype(F32) * _dot(o_ref[...], wba_ref[...]))
    out_ref[...] = x_ref[...] + _dot(m.astype(BF16), wo_ref[...])


def _merge(x, y, o, sr, sa, wbr, wba, wo, tm):
    rows, d = x.shape
    tm = _row_tile(rows, tm)
    return pl.pallas_call(
        _merge_body,
        grid=(rows // tm,),
        in_specs=[_row_spec(tm, d)] * 5 + [_resident((d, d))] * 3,
        out_specs=_row_spec(tm, d),
        out_shape=jax.ShapeDtypeStruct((rows, d), F32),
        compiler_params=_params(1),
        name="merge",
    )(x, y, o, sr, sa, wbr, wba, wo)


def _block_diag(w):
    n, c, _ = w.shape
    eye = jnp.eye(n, dtype=w.dtype)
    return (eye[:, None, :, None] * w[:, :, None, :]).reshape(n * c, n * c)


def _layer(x, p_all, lw, k_past, v_past, h0, cbuf, k_all, v_all, layer, depth, pos0, final, tm, tt, tq,
           seqs):
    b, s, d = x.shape
    rows = b * s
    x = x.reshape(rows, d)
    x = _ffn(x, lw["n_ffn1"], lw["f1_g"], lw["f1_u"], lw["f1_d"], tm)
    ux, gg, q, kb, vb, sr, sa, k_all, v_all = _inproj(x, lw["n_mix"], lw["w_in"], k_all, v_all,
                                                      layer, depth, b, s, tm)
    shp = (b, s, d)
    y, h_last, new_conv = _rnn(ux.reshape(shp), gg.reshape(shp), lw["conv_w"], lw["conv_b"],
                               lw["wa"], lw["ba"], lw["wx"], lw["bx"], lw["lam"], h0, cbuf,
                               pos0, tt)
    kb, vb = kb.reshape(shp), vb.reshape(shp)
    if k_past is not None:
        past = k_past.shape[1]
        tk = 256
        fill = (-(past + s)) % tk
        zeros = jnp.zeros((b, fill, d), BF16)
        kb = jnp.concatenate([k_past.reshape(b, past, d).astype(BF16), kb, zeros], axis=1)
        vb = jnp.concatenate([v_past.reshape(b, past, d).astype(BF16), vb, zeros], axis=1)
    o = _attn(q.reshape(shp), kb, vb, pos0, tq, seqs)
    x = _merge(x, y.reshape(rows, d), o.reshape(rows, d), sr, sa,
               lw["w_br"], lw["w_ba"], lw["w_out"], tm)
    x = _ffn(x, lw["n_ffn2"], lw["f2_g"], lw["f2_u"], lw["f2_d"], tm,
             ple=(p_all.reshape(depth, rows, -1), layer, lw["n_ple"], lw["ple_g"], lw["ple_p"],
                  lw["final"], final))
    return x.reshape(shp), k_all, v_all, h_last.reshape(b, d), new_conv


def kernel(x_prompt, x_sample, p_prompt, p_sample, cache_k, cache_v, state_h, state_conv, norm_ffn1, ffn1_w_gate, ffn1_w_up, ffn1_w_down, norm_mix, w_in, conv_w, conv_b, lru_w_a, lru_b_a, lru_w_x, lru_b_x, lru_lambda, w_branch_rnn, w_branch_attn, w_out, norm_ffn2, ffn2_w_gate, ffn2_w_up, ffn2_w_down, norm_ple, ple_w_gate, ple_w_proj, final_norm):
    depth = w_in.shape[0]
    b, s, d = x_prompt.shape
    bd, sd, _ = x_sample.shape
    past = cache_k.shape[2]
    row = lambda a: a.reshape(1, -1)
    xp, xs = x_prompt, x_sample
    kp = vp = ks = vs = None
    outs = [[] for _ in range(4)]
    for i in range(depth):
        lw = dict(
            n_ffn1=row(norm_ffn1[i]), f1_g=ffn1_w_gate[i].astype(BF16),
            f1_u=ffn1_w_up[i].astype(BF16), f1_d=ffn1_w_down[i].astype(BF16),
            n_mix=row(norm_mix[i]), w_in=w_in[i].astype(BF16),
            conv_w=conv_w[i], conv_b=row(conv_b[i]),
            wa=_block_diag(lru_w_a[i]).astype(BF16), ba=row(lru_b_a[i]),
            wx=_block_diag(lru_w_x[i]).astype(BF16), bx=row(lru_b_x[i]),
            lam=row(lru_lambda[i]),
            w_br=w_branch_rnn[i].astype(BF16), w_ba=w_branch_attn[i].astype(BF16),
            w_out=w_out[i].astype(BF16),
            n_ffn2=row(norm_ffn2[i]), f2_g=ffn2_w_gate[i].astype(BF16),
            f2_u=ffn2_w_up[i].astype(BF16), f2_d=ffn2_w_down[i].astype(BF16),
            n_ple=row(norm_ple[i]), ple_g=ple_w_gate[i].astype(BF16),
            ple_p=ple_w_proj[i].astype(BF16), final=row(final_norm))
        final = i == depth - 1
        xp, kp, vp, hp, cp = _layer(
            xp, p_prompt, lw, None, None, jnp.zeros((b, 1, d), F32),
            jnp.zeros((b, CONV_W - 1, d), F32), kp, vp, i, depth, 0, final,
            tm=512, tt=512, tq=256, seqs=1)
        xs, ks, vs, hs, cs = _layer(
            xs, p_sample, lw, cache_k[i], cache_v[i], state_h[i].reshape(bd, 1, d),
            state_conv[i], ks, vs, i, depth, past, final, tm=bd * sd, tt=sd, tq=sd, seqs=bd)
        for lst, val in zip(outs, (hp, cp, hs, cs)):
            lst.append(val)
    hp, cp, hs, cs = (jnp.stack(lst) for lst in outs)
    seq_major = lambda a: jnp.transpose(a, (0, 1, 4, 2, 3))
    return (xp, xs, seq_major(kp), seq_major(vp), hp, cp, seq_major(ks), seq_major(vs), hs, cs)
```

```python
import functools

import jax
import jax.numpy as jnp
from jax import lax
from jax.experimental import pallas as pl
from jax.experimental.pallas import tpu as pltpu

N_HEADS = 16
HEAD_DIM = 64
N_LRU_BLOCKS = 16
LRU_C = 8.0
CONV_W = 4
RMS_EPS = 1e-6
N_IN_GROUPS = 7

LANES = 128
HEADS_PER_STEP = LANES // HEAD_DIM
VMEM_LIMIT = 56 * 1024 * 1024

F32 = jnp.float32
BF16 = jnp.bfloat16


def _rms(x, g):
    return x * lax.rsqrt(jnp.mean(x * x, axis=-1, keepdims=True) + RMS_EPS) * g


def _dot(a, b):
    return jnp.dot(a, b, preferred_element_type=F32)


def _row_spec(tm, d):
    return pl.BlockSpec((tm, d), lambda i: (i, 0))


def _resident(shape):
    return pl.BlockSpec(shape, lambda *_: (0,) * len(shape), pipeline_mode=pl.Buffered(1))


def _params(n_axes):
    return pltpu.CompilerParams(dimension_semantics=("arbitrary",) * n_axes,
                                vmem_limit_bytes=VMEM_LIMIT)


def _row_tile(rows, want):
    tm = min(rows, want)
    assert rows % tm == 0
    return tm


def _ffn_body(with_ple, final, x_ref, g_ref, wg_ref, wu_ref, wd_ref, *rest):
    o_ref = rest[-1]
    x = x_ref[...]
    h = _rms(x, g_ref[...]).astype(BF16)
    gate = _dot(h, wg_ref[...])
    up = _dot(h, wu_ref[...])
    act = (gate * jax.nn.sigmoid(gate) * up).astype(BF16)
    x = x + 0.5 * _dot(act, wd_ref[...])
    if with_ple:
        p_ref, pg_ref, pwg_ref, pwp_ref, fg_ref = rest[:-1]
        gate = jax.nn.sigmoid(_dot(_rms(x, pg_ref[...]).astype(BF16), pwg_ref[...]))
        x = x + gate * _dot(p_ref[0].astype(BF16), pwp_ref[...])
        if final:
            x = _rms(x, fg_ref[...])
    o_ref[...] = x


def _ffn(x, g, wg, wu, wd, tm, ple=None):
    rows, d = x.shape
    dff = wg.shape[1]
    tm = _row_tile(rows, tm)
    in_specs = [_row_spec(tm, d), _resident((1, d)), _resident((d, dff)),
                _resident((d, dff)), _resident((dff, d))]
    args = [x, g, wg, wu, wd]
    final = False
    if ple is not None:
        p_all, layer, pg, pwg, pwp, fg, final = ple
        dp = p_all.shape[-1]
        in_specs += [pl.BlockSpec((1, tm, dp), lambda i: (layer, i, 0)), _resident((1, d)),
                     _resident((d, d)), _resident((dp, d)), _resident((1, d))]
        args += [p_all, pg, pwg, pwp, fg]
    return pl.pallas_call(
        functools.partial(_ffn_body, ple is not None, final),
        grid=(rows // tm,),
        in_specs=in_specs,
        out_specs=_row_spec(tm, d),
        out_shape=jax.ShapeDtypeStruct((rows, d), F32),
        compiler_params=_params(1),
        name="ffn",
    )(*args)


def _inproj_body(x_ref, g_ref, w_ref, k_prev, v_prev, ux_ref, gg_ref, q_ref, kb_ref, vb_ref, sr_ref,
                 sa_ref, kt_ref, vt_ref):
    del k_prev, v_prev
    d = x_ref.shape[1]
    h = _rms(x_ref[...], g_ref[...]).astype(BF16)

    def col(i):
        return _dot(h, w_ref[:, i * d:(i + 1) * d])

    def put_transposed(ref, val):
        n_seq, _, _, s = ref.shape[1:]
        vt = val.T
        for b in range(n_seq):
            ref[0, b] = vt[:, b * s:(b + 1) * s].reshape(N_HEADS, HEAD_DIM, s)

    ux_ref[...] = col(0)
    gg_ref[...] = jax.nn.gelu(col(1)).astype(BF16)
    q_ref[...] = (col(2) * (HEAD_DIM ** -0.5)).astype(BF16)
    k = col(3)
    put_transposed(kt_ref, k)
    kb_ref[...] = k.astype(BF16)
    v = col(4)
    put_transposed(vt_ref, v)
    vb_ref[...] = v.astype(BF16)
    sr_ref[...] = jax.nn.sigmoid(col(5)).astype(BF16)
    sa_ref[...] = jax.nn.sigmoid(col(6)).astype(BF16)


def _inproj(x, g, w_in, k_all, v_all, layer, depth, b, s, tm):
    rows, d = x.shape
    tm = _row_tile(rows, tm)
    assert tm % s == 0 or s % tm == 0
    n_seq, ts = max(tm // s, 1), min(tm, s)
    kv_shape = jax.ShapeDtypeStruct((depth, b, N_HEADS, HEAD_DIM, s), F32)
    kv_spec = pl.BlockSpec((1, n_seq, N_HEADS, HEAD_DIM, ts),
                           lambda i: (layer, i * tm // s, 0, 0, i % (s // ts)))
    dts = (F32, BF16, BF16, BF16, BF16, BF16, BF16)
    if k_all is None:
        k_all = v_all = jnp.zeros((1,), F32)
        aliases = {}
    else:
        aliases = {3: len(dts), 4: len(dts) + 1}
    return pl.pallas_call(
        _inproj_body,
        grid=(rows // tm,),
        in_specs=[_row_spec(tm, d), _resident((1, d)), _resident((d, N_IN_GROUPS * d)),
                  pl.BlockSpec(memory_space=pl.ANY), pl.BlockSpec(memory_space=pl.ANY)],
        out_specs=[_row_spec(tm, d)] * len(dts) + [kv_spec, kv_spec],
        out_shape=[jax.ShapeDtypeStruct((rows, d), t) for t in dts] + [kv_shape, kv_shape],
        input_output_aliases=aliases,
        compiler_params=_params(1),
        name="inproj",
    )(x, g, w_in, k_all, v_all)


def _rnn_body(pos0, ux_ref, gg_ref, cw_ref, cb_ref, wa_ref, ba_ref, wx_ref, bx_ref, lam_ref,
              h0_ref, cbuf_ref, y_ref, hl_ref, nc_ref, ubuf, a_s, b_s, h_s, hc):
    t_idx = pl.program_id(1)
    tt = ux_ref.shape[1]
    pad = ubuf.shape[0] - tt
    hist = CONV_W - 1

    @pl.when(t_idx == 0)
    def _():
        ubuf[pad - hist:pad, :] = cbuf_ref[0]
        hc[...] = h0_ref[0]

    @pl.when(t_idx != 0)
    def _():
        ubuf[0:pad, :] = ubuf[tt:tt + pad, :]

    u = ux_ref[0]
    ubuf[pad:pad + tt, :] = u
    first = lax.broadcasted_iota(jnp.int32, (8, 1), 0) == 0

    def shift_down(x, row_before):
        rolled = pltpu.roll(x, 1, 0)
        return jnp.concatenate([jnp.where(first, row_before, rolled[0:8]), rolled[8:]], axis=0)

    w = [cw_ref[j:j + 1, :] for j in range(CONV_W)]
    um1, um2, um3 = (ubuf[pad - j:pad - j + 1, :] for j in (1, 2, 3))
    acc = shift_down(u * w[0], um1 * w[0])
    acc = shift_down(u * w[1] + acc, um1 * w[1] + um2 * w[0])
    acc = shift_down(u * w[2] + acc, um1 * w[2] + (um2 * w[1] + um3 * w[0]))
    xc = cb_ref[...] + (u * w[3] + acc)

    xb = xc.astype(BF16)
    r = jax.nn.sigmoid(_dot(xb, wa_ref[...]) + ba_ref[...])
    gate_i = jax.nn.sigmoid(_dot(xb, wx_ref[...]) + bx_ref[...])
    lam = lam_ref[...]
    softplus_neg_lam = jnp.maximum(-lam, 0.0) + jnp.log1p(jnp.exp(-jnp.abs(lam)))
    log_a = (-LRU_C * softplus_neg_lam) * r
    a_s[...] = jnp.exp(log_a)
    th = jnp.tanh(log_a)
    sq = -2.0 * th / (1.0 - th)
    gated = gate_i * xc
    b_s[...] = jnp.where(sq > 0.0, sq * lax.rsqrt(sq), 0.0) * gated
    if pos0 == 0:
        @pl.when(t_idx == 0)
        def _():
            a_s[0:1, :] = jnp.zeros((1, a_s.shape[1]), F32)
            b_s[0:1, :] = gated[0:1, :]

    def step(t, h):
        h = a_s[pl.ds(t, 1), :] * h + b_s[pl.ds(t, 1), :]
        h_s[pl.ds(t, 1), :] = h
        return h

    h_end = lax.fori_loop(0, tt, step, hc[...], unroll=8)
    hc[...] = h_end
    hl_ref[0] = h_end
    nc_ref[0] = ubuf[pad + tt - hist:pad + tt, :]
    y_ref[0] = (h_s[...] * gg_ref[0].astype(F32)).astype(BF16)


def _rnn(ux, gg, cw, cb, wa, ba, wx, bx, lam, h0, cbuf, pos0, tt):
    b, s, d = ux.shape
    tt = _row_tile(s, tt)
    pad = 8
    tile = pl.BlockSpec((1, tt, d), lambda i, t: (i, t, 0))
    per_b = lambda n: pl.BlockSpec((1, n, d), lambda i, t: (i, 0, 0))
    return pl.pallas_call(
        functools.partial(_rnn_body, pos0),
        grid=(b, s // tt),
        in_specs=[tile, tile, _resident((CONV_W, d)), _resident((1, d)), _resident((d, d)),
                  _resident((1, d)), _resident((d, d)), _resident((1, d)), _resident((1, d)),
                  per_b(1), per_b(CONV_W - 1)],
        out_specs=[tile, per_b(1), per_b(CONV_W - 1)],
        out_shape=[jax.ShapeDtypeStruct((b, s, d), BF16),
                   jax.ShapeDtypeStruct((b, 1, d), F32),
                   jax.ShapeDtypeStruct((b, CONV_W - 1, d), F32)],
        scratch_shapes=[pltpu.VMEM((tt + pad, d), F32), pltpu.VMEM((tt, d), F32),
                        pltpu.VMEM((tt, d), F32), pltpu.VMEM((tt, d), F32),
                        pltpu.VMEM((1, d), F32)],
        compiler_params=_params(2),
        name="rnn",
    )(ux, gg, cw, cb, wa, ba, wx, bx, lam, h0, cbuf)


MASKED_SCORE = -1e30
N_STREAMS = 4
ZERO_WEIGHT_AT = 104.0


def _softplus(z):
    neg_abs = lax.bitcast_convert_type(
        lax.bitcast_convert_type(z, jnp.uint32) | jnp.uint32(0x80000000), F32)
    return jnp.maximum(z, 0.0) + jnp.log(1.0 + jnp.exp(neg_abs))


def _attn_body(q_pos0, n_past, q_ref, k_ref, v_ref, *rest):
    if n_past:
        kp_ref, vp_ref = rest[:2]
        rest = rest[2:]
    o_ref, bias_s, tri_s, z_s, sp_s, c_s, w_s, carry_s, acc_s = rest
    n_seq, sq, _ = q_ref.shape
    tq, tk = bias_s.shape[1:]
    nq_seq = sq // tq
    n_q = n_seq * nq_seq
    nh = HEADS_PER_STEP

    row = lax.broadcasted_iota(jnp.int32, (tk, tk), 0)
    col = lax.broadcasted_iota(jnp.int32, (tk, tk), 1)
    tri_s[...] = jnp.where(row >= col, 1.0, 0.0).astype(BF16)
    qrow = q_pos0 % tk + lax.broadcasted_iota(jnp.int32, (tq, tk), 0)
    kcol = lax.broadcasted_iota(jnp.int32, (tq, tk), 1)
    bias_s[0] = jnp.zeros((tq, tk), F32)
    bias_s[1] = jnp.where(kcol < qrow, 0.0, MASKED_SCORE)
    for ref in (z_s, sp_s, c_s, w_s, carry_s, acc_s, o_ref):
        ref[...] = jnp.zeros(ref.shape, ref.dtype)

    def place(qi):
        return qi // nq_seq, pl.multiple_of((qi % nq_seq) * tq, tq)

    def diag_tile(qi):
        return (q_pos0 + (qi % nq_seq) * tq) // tk

    def key_tile(qi, p):
        tile = jnp.maximum(diag_tile(qi) - p, 0)
        new = pl.ds(pl.multiple_of(jnp.maximum(tile - n_past, 0) * tk, tk), tk)
        old = pl.ds(pl.multiple_of(jnp.minimum(tile, max(n_past - 1, 0)) * tk, tk), tk)
        return qi // nq_seq, tile >= n_past, new, old

    def head(x, h):
        return x[:, h * HEAD_DIM:(h + 1) * HEAD_DIM]

    def nt_dot(a, b):
        return lax.dot_general(a, b, (((1,), (1,)), ((), ())), preferred_element_type=F32)

    def scores(q2, qi, p):
        seq, is_new, new, old = key_tile(qi, p)
        k2 = k_ref[seq, new, :]
        z = [nt_dot(head(q2, h), head(k2, h)) for h in range(nh)]
        if n_past:
            z = [jnp.where(is_new, z[h], _dot(head(q2, h), kp_ref[0, seq, h, :, old].astype(BF16)))
                 for h in range(nh)]
        return z

    def weighted_values(w, qi, p):
        seq, is_new, new, old = key_tile(qi, p)
        v2 = v_ref[seq, new, :]
        pv = [_dot(w[h], head(v2, h)) for h in range(nh)]
        if n_past:
            pv = [jnp.where(is_new, pv[h], nt_dot(w[h], vp_ref[0, seq, h, :, old].astype(BF16)))
                  for h in range(nh)]
        return pv

    def sub_step(u, streams, next_q):
        qi, p, real, spent = streams[u]

        seq, r0 = place(qi)
        pv = weighted_values([w_s[u, h] for h in range(nh)], qi, p)
        out = []
        for h in range(nh):
            a = jnp.where(p == 0, 0.0, acc_s[u, h]) + pv[h]
            acc_s[u, h] = a
            out.append(a)
        o_ref[seq, pl.ds(r0, tq), :] = jnp.where(
            real == 1, jnp.concatenate(out, axis=1).astype(BF16), o_ref[seq, pl.ds(r0, tq), :])

        t = (u + 1) % N_STREAMS
        qi_t, p_t, real_t, _ = streams[t]
        low = None
        for h in range(nh):
            c = c_s[t, h]
            base = jnp.where(p_t == 0, 0.0, carry_s[t, h])
            base = jnp.where(real_t == 1, base, -MASKED_SCORE)
            w_s[t, h] = jnp.exp(z_s[t, h] - c - base).astype(BF16)
            total = base + c[:, 0:1]
            carry_s[t, h] = total
            low = jnp.min(total) if low is None else jnp.minimum(low, jnp.min(total))
        streams[t] = (qi_t, p_t, real_t, (low > ZERO_WEIGHT_AT).astype(jnp.int32))

        t = (u + 2) % N_STREAMS
        for h in range(nh):
            c_s[t, h] = _dot(sp_s[t, h], tri_s[...])

        t = (u + 3) % N_STREAMS
        for h in range(nh):
            sp_s[t, h] = _softplus(z_s[t, h]).astype(BF16)

        done = jnp.logical_or(jnp.logical_or(real == 0, p >= diag_tile(qi)), spent == 1)
        take = jnp.logical_and(done, next_q < n_q)
        qi = jnp.where(take, next_q, qi)
        p = jnp.where(done, jnp.where(take, 0, 1), p + 1)
        real = jnp.where(done, take.astype(jnp.int32), 1)
        next_q = next_q + take.astype(jnp.int32)
        streams[u] = (qi, p, real, jnp.int32(0))

        seq, r0 = place(qi)
        q2 = q_ref[seq, pl.ds(r0, tq), :]
        bias = bias_s[jnp.where(p == 0, 1, 0)]
        z = scores(q2, qi, p)
        for h in range(nh):
            z_s[u, h] = bias + z[h]
        return streams, next_q

    def one_round(state):
        streams, next_q = list(state[0]), state[1]
        for u in range(N_STREAMS):
            streams, next_q = sub_step(u, streams, next_q)
        return tuple(streams), next_q

    def work_left(state):
        streams, next_q = state
        busy = next_q < n_q
        for _, _, real, _ in streams:
            busy = jnp.logical_or(busy, real == 1)
        return busy

    idle = (jnp.int32(0), jnp.int32(1), jnp.int32(0), jnp.int32(1))
    lax.while_loop(work_left, one_round, ((idle,) * N_STREAMS, jnp.int32(0)))


def _attn(q, k, v, tq, seqs_per_step, past=None):
    b, sq, d = q.shape
    sk = k.shape[1]
    tq = _row_tile(sq, tq)
    tk = min(sk, max(tq, 256))
    nb = seqs_per_step
    q_pos0 = 0 if past is None else past[0].shape[-1]
    n_past = q_pos0 // tk
    assert sk % tk == 0 and q_pos0 % tk == 0 and sq <= sk and (tq == tk or sq == tq)
    assert b % nb == 0
    n_col = d // LANES
    nh = HEADS_PER_STEP
    q_spec = pl.BlockSpec((nb, sq, LANES), lambda i, c: (i, 0, c))
    kv_spec = pl.BlockSpec((nb, sk, LANES), lambda i, c: (i, 0, c))
    in_specs, args = [q_spec, kv_spec, kv_spec], [q, k, v]
    if past is not None:
        layer = past[2]
        in_specs += [pl.BlockSpec((1, nb, nh, HEAD_DIM, q_pos0),
                                  lambda i, c: (layer, i, c, 0, 0))] * 2
        args += list(past[:2])
    ring = lambda width, dt: pltpu.VMEM((N_STREAMS, nh, tq, width), dt)
    return pl.pallas_call(
        functools.partial(_attn_body, q_pos0, n_past),
        grid=(b // nb, n_col),
        in_specs=in_specs,
        out_specs=q_spec,
        out_shape=jax.ShapeDtypeStruct((b, sq, d), BF16),
        scratch_shapes=[pltpu.VMEM((2, tq, tk), F32), pltpu.VMEM((tk, tk), BF16),
                        ring(tk, F32), ring(tk, BF16), ring(tk, F32), ring(tk, BF16),
                        ring(1, F32), ring(HEAD_DIM, F32)],
        compiler_params=_params(2),
        name="attn",
    )(*args)


def _merge_body(x_ref, y_ref, o_ref, sr_ref, sa_ref, wbr_ref, wba_ref, wo_ref, out_ref):
    m = (sr_ref[...].astype(F32) * _dot(y_ref[...], wbr_ref[...])
         + sa_ref[...].astype(F32) * _dot(o_ref[...], wba_ref[...]))
    out_ref[...] = x_ref[...] + _dot(m.astype(BF16), wo_ref[...])


def _merge(x, y, o, sr, sa, wbr, wba, wo, tm):
    rows, d = x.shape
    tm = _row_tile(rows, tm)
    return pl.pallas_call(
        _merge_body,
        grid=(rows // tm,),
        in_specs=[_row_spec(tm, d)] * 5 + [_resident((d, d))] * 3,
        out_specs=_row_spec(tm, d),
        out_shape=jax.ShapeDtypeStruct((rows, d), F32),
        compiler_params=_params(1),
        name="merge",
    )(x, y, o, sr, sa, wbr, wba, wo)


def _block_diag(w):
    n, c, _ = w.shape
    eye = jnp.eye(n, dtype=w.dtype)
    return (eye[:, None, :, None] * w[:, :, None, :]).reshape(n * c, n * c)


def _layer(x, p_all, lw, past, h0, cbuf, k_all, v_all, layer, depth, final, tm, tt, tq, seqs):
    b, s, d = x.shape
    rows = b * s
    x = x.reshape(rows, d)
    x = _ffn(x, lw["n_ffn1"], lw["f1_g"], lw["f1_u"], lw["f1_d"], tm)
    ux, gg, q, kb, vb, sr, sa, k_all, v_all = _inproj(x, lw["n_mix"], lw["w_in"], k_all, v_all,
                                                      layer, depth, b, s, tm)
    shp = (b, s, d)
    y, h_last, new_conv = _rnn(ux.reshape(shp), gg.reshape(shp), lw["conv_w"], lw["conv_b"],
                               lw["wa"], lw["ba"], lw["wx"], lw["bx"], lw["lam"], h0, cbuf,
                               0 if past is None else past[0].shape[-1], tt)
    kb, vb = kb.reshape(shp), vb.reshape(shp)
    if past is not None:
        fill = ((0, 0), (0, (-s) % 256), (0, 0))
        kb, vb = jnp.pad(kb, fill), jnp.pad(vb, fill)
    o = _attn(q.reshape(shp), kb, vb, tq, seqs, past)
    x = _merge(x, y.reshape(rows, d), o.reshape(rows, d), sr, sa,
               lw["w_br"], lw["w_ba"], lw["w_out"], tm)
    x = _ffn(x, lw["n_ffn2"], lw["f2_g"], lw["f2_u"], lw["f2_d"], tm,
             ple=(p_all.reshape(depth, rows, -1), layer, lw["n_ple"], lw["ple_g"], lw["ple_p"],
                  lw["final"], final))
    return x.reshape(shp), k_all, v_all, h_last.reshape(b, d), new_conv


def kernel(x_prompt, x_sample, p_prompt, p_sample, cache_k, cache_v, state_h, state_conv, norm_ffn1, ffn1_w_gate, ffn1_w_up, ffn1_w_down, norm_mix, w_in, conv_w, conv_b, lru_w_a, lru_b_a, lru_w_x, lru_b_x, lru_lambda, w_branch_rnn, w_branch_attn, w_out, norm_ffn2, ffn2_w_gate, ffn2_w_up, ffn2_w_down, norm_ple, ple_w_gate, ple_w_proj, final_norm):
    depth = w_in.shape[0]
    b, s, d = x_prompt.shape
    bd, sd, _ = x_sample.shape
    keys_minor = lambda a: jnp.transpose(a, (0, 1, 3, 4, 2))
    row = lambda a: a.reshape(1, -1)
    xp, xs = x_prompt, x_sample
    kp = vp = ks = vs = None
    outs = [[] for _ in range(4)]
    for i in range(depth):
        lw = dict(
            n_ffn1=row(norm_ffn1[i]), f1_g=ffn1_w_gate[i].astype(BF16),
            f1_u=ffn1_w_up[i].astype(BF16), f1_d=ffn1_w_down[i].astype(BF16),
            n_mix=row(norm_mix[i]), w_in=w_in[i].astype(BF16),
            conv_w=conv_w[i], conv_b=row(conv_b[i]),
            wa=_block_diag(lru_w_a[i]).astype(BF16), ba=row(lru_b_a[i]),
            wx=_block_diag(lru_w_x[i]).astype(BF16), bx=row(lru_b_x[i]),
            lam=row(lru_lambda[i]),
            w_br=w_branch_rnn[i].astype(BF16), w_ba=w_branch_attn[i].astype(BF16),
            w_out=w_out[i].astype(BF16),
            n_ffn2=row(norm_ffn2[i]), f2_g=ffn2_w_gate[i].astype(BF16),
            f2_u=ffn2_w_up[i].astype(BF16), f2_d=ffn2_w_down[i].astype(BF16),
            n_ple=row(norm_ple[i]), ple_g=ple_w_gate[i].astype(BF16),
            ple_p=ple_w_proj[i].astype(BF16), final=row(final_norm))
        final = i == depth - 1
        xp, kp, vp, hp, cp = _layer(
            xp, p_prompt, lw, None, jnp.zeros((b, 1, d), F32),
            jnp.zeros((b, CONV_W - 1, d), F32), kp, vp, i, depth, final,
            tm=512, tt=512, tq=256, seqs=1)
        xs, ks, vs, hs, cs = _layer(
            xs, p_sample, lw, (keys_minor(cache_k), keys_minor(cache_v), i), state_h[i].reshape(bd, 1, d),
            state_conv[i], ks, vs, i, depth, final, tm=bd * sd, tt=sd, tq=sd, seqs=bd)
        for lst, val in zip(outs, (hp, cp, hs, cs)):
            lst.append(val)
    hp, cp, hs, cs = (jnp.stack(lst) for lst in outs)
    seq_major = lambda a: jnp.transpose(a, (0, 1, 4, 2, 3))
    return (xp, xs, seq_major(kp), seq_major(vp), hp, cp, seq_major(ks), seq_major(vs), hs, cs)
```

```python
import functools

import jax
import jax.numpy as jnp
from jax import lax
from jax.experimental import pallas as pl
from jax.experimental.pallas import tpu as pltpu

N_HEADS = 16
HEAD_DIM = 64
N_LRU_BLOCKS = 16
LRU_C = 8.0
CONV_W = 4
RMS_EPS = 1e-6
N_IN_GROUPS = 7

LANES = 128
HEADS_PER_STEP = LANES // HEAD_DIM
VMEM_LIMIT = 56 * 1024 * 1024

F32 = jnp.float32
BF16 = jnp.bfloat16


def _rms(x, g):
    return x * lax.rsqrt(jnp.mean(x * x, axis=-1, keepdims=True) + RMS_EPS) * g


def _dot(a, b):
    return jnp.dot(a, b, preferred_element_type=F32)


def _row_spec(tm, d):
    return pl.BlockSpec((tm, d), lambda i: (i, 0))


def _resident(shape):
    return pl.BlockSpec(shape, lambda *_: (0,) * len(shape), pipeline_mode=pl.Buffered(1))


def _params(n_axes):
    return pltpu.CompilerParams(dimension_semantics=("arbitrary",) * n_axes,
                                vmem_limit_bytes=VMEM_LIMIT)


def _row_tile(rows, want):
    tm = min(rows, want)
    assert rows % tm == 0
    return tm


def _ffn_body(with_ple, final, x_ref, g_ref, wg_ref, wu_ref, wd_ref, *rest):
    o_ref = rest[-1]
    x = x_ref[...]
    h = _rms(x, g_ref[...]).astype(BF16)
    gate = _dot(h, wg_ref[...])
    up = _dot(h, wu_ref[...])
    act = (gate * jax.nn.sigmoid(gate) * up).astype(BF16)
    x = x + 0.5 * _dot(act, wd_ref[...])
    if with_ple:
        p_ref, pg_ref, pwg_ref, pwp_ref, fg_ref = rest[:-1]
        gate = jax.nn.sigmoid(_dot(_rms(x, pg_ref[...]).astype(BF16), pwg_ref[...]))
        x = x + gate * _dot(p_ref[0].astype(BF16), pwp_ref[...])
        if final:
            x = _rms(x, fg_ref[...])
    o_ref[...] = x


def _ffn(x, g, wg, wu, wd, tm, ple=None):
    rows, d = x.shape
    dff = wg.shape[1]
    tm = _row_tile(rows, tm)
    in_specs = [_row_spec(tm, d), _resident((1, d)), _resident((d, dff)),
                _resident((d, dff)), _resident((dff, d))]
    args = [x, g, wg, wu, wd]
    final = False
    if ple is not None:
        p_all, layer, pg, pwg, pwp, fg, final = ple
        dp = p_all.shape[-1]
        in_specs += [pl.BlockSpec((1, tm, dp), lambda i: (layer, i, 0)), _resident((1, d)),
                     _resident((d, d)), _resident((dp, d)), _resident((1, d))]
        args += [p_all, pg, pwg, pwp, fg]
    return pl.pallas_call(
        functools.partial(_ffn_body, ple is not None, final),
        grid=(rows // tm,),
        in_specs=in_specs,
        out_specs=_row_spec(tm, d),
        out_shape=jax.ShapeDtypeStruct((rows, d), F32),
        compiler_params=_params(1),
        name="ffn",
    )(*args)


def _inproj_body(x_ref, g_ref, w_ref, k_prev, v_prev, ux_ref, gg_ref, q_ref, kb_ref, vb_ref, sr_ref,
                 sa_ref, kt_ref, vt_ref):
    del k_prev, v_prev
    d = x_ref.shape[1]
    h = _rms(x_ref[...], g_ref[...]).astype(BF16)

    def col(i):
        return _dot(h, w_ref[:, i * d:(i + 1) * d])

    def put_transposed(ref, val):
        n_seq, _, _, s = ref.shape[1:]
        vt = val.T
        for b in range(n_seq):
            ref[0, b] = vt[:, b * s:(b + 1) * s].reshape(N_HEADS, HEAD_DIM, s)

    ux_ref[...] = col(0)
    gg_ref[...] = jax.nn.gelu(col(1)).astype(BF16)
    q_ref[...] = (col(2) * (HEAD_DIM ** -0.5)).astype(BF16)
    k = col(3)
    put_transposed(kt_ref, k)
    kb_ref[...] = k.astype(BF16)
    v = col(4)
    put_transposed(vt_ref, v)
    vb_ref[...] = v.astype(BF16)
    sr_ref[...] = jax.nn.sigmoid(col(5)).astype(BF16)
    sa_ref[...] = jax.nn.sigmoid(col(6)).astype(BF16)


def _inproj(x, g, w_in, k_all, v_all, layer, depth, b, s, tm):
    rows, d = x.shape
    tm = _row_tile(rows, tm)
    assert tm % s == 0 or s % tm == 0
    n_seq, ts = max(tm // s, 1), min(tm, s)
    kv_shape = jax.ShapeDtypeStruct((depth, b, N_HEADS, HEAD_DIM, s), F32)
    kv_spec = pl.BlockSpec((1, n_seq, N_HEADS, HEAD_DIM, ts),
                           lambda i: (layer, i * tm // s, 0, 0, i % (s // ts)))
    dts = (F32, BF16, BF16, BF16, BF16, BF16, BF16)
    if k_all is None:
        k_all = v_all = jnp.zeros((1,), F32)
        aliases = {}
    else:
        aliases = {3: len(dts), 4: len(dts) + 1}
    return pl.pallas_call(
        _inproj_body,
        grid=(rows // tm,),
        in_specs=[_row_spec(tm, d), _resident((1, d)), _resident((d, N_IN_GROUPS * d)),
                  pl.BlockSpec(memory_space=pl.ANY), pl.BlockSpec(memory_space=pl.ANY)],
        out_specs=[_row_spec(tm, d)] * len(dts) + [kv_spec, kv_spec],
        out_shape=[jax.ShapeDtypeStruct((rows, d), t) for t in dts] + [kv_shape, kv_shape],
        input_output_aliases=aliases,
        compiler_params=_params(1),
        name="inproj",
    )(x, g, w_in, k_all, v_all)


def _rnn_body(pos0, ux_ref, gg_ref, cw_ref, cb_ref, wa_ref, ba_ref, wx_ref, bx_ref, lam_ref,
              h0_ref, cbuf_ref, y_ref, hl_ref, nc_ref, ubuf, a_s, b_s, h_s, hc):
    t_idx = pl.program_id(1)
    tt = ux_ref.shape[1]
    pad = ubuf.shape[0] - tt
    hist = CONV_W - 1

    @pl.when(t_idx == 0)
    def _():
        ubuf[pad - hist:pad, :] = cbuf_ref[0]
        hc[...] = h0_ref[0]

    @pl.when(t_idx != 0)
    def _():
        ubuf[0:pad, :] = ubuf[tt:tt + pad, :]

    u = ux_ref[0]
    ubuf[pad:pad + tt, :] = u
    first = lax.broadcasted_iota(jnp.int32, (8, 1), 0) == 0

    def shift_down(x, row_before):
        rolled = pltpu.roll(x, 1, 0)
        return jnp.concatenate([jnp.where(first, row_before, rolled[0:8]), rolled[8:]], axis=0)

    w = [cw_ref[j:j + 1, :] for j in range(CONV_W)]
    um1, um2, um3 = (ubuf[pad - j:pad - j + 1, :] for j in (1, 2, 3))
    acc = shift_down(u * w[0], um1 * w[0])
    acc = shift_down(u * w[1] + acc, um1 * w[1] + um2 * w[0])
    acc = shift_down(u * w[2] + acc, um1 * w[2] + (um2 * w[1] + um3 * w[0]))
    xc = cb_ref[...] + (u * w[3] + acc)

    xb = xc.astype(BF16)
    r = jax.nn.sigmoid(_dot(xb, wa_ref[...]) + ba_ref[...])
    gate_i = jax.nn.sigmoid(_dot(xb, wx_ref[...]) + bx_ref[...])
    lam = lam_ref[...]
    softplus_neg_lam = jnp.maximum(-lam, 0.0) + jnp.log1p(jnp.exp(-jnp.abs(lam)))
    log_a = (-LRU_C * softplus_neg_lam) * r
    a_s[...] = jnp.exp(log_a)
    th = jnp.tanh(log_a)
    sq = -2.0 * th / (1.0 - th)
    gated = gate_i * xc
    b_s[...] = jnp.where(sq > 0.0, sq * lax.rsqrt(sq), 0.0) * gated
    if pos0 == 0:
        @pl.when(t_idx == 0)
        def _():
            a_s[0:1, :] = jnp.zeros((1, a_s.shape[1]), F32)
            b_s[0:1, :] = gated[0:1, :]

    def step(t, h):
        h = a_s[pl.ds(t, 1), :] * h + b_s[pl.ds(t, 1), :]
        h_s[pl.ds(t, 1), :] = h
        return h

    h_end = lax.fori_loop(0, tt, step, hc[...], unroll=8)
    hc[...] = h_end
    hl_ref[0] = h_end
    nc_ref[0] = ubuf[pad + tt - hist:pad + tt, :]
    y_ref[0] = (h_s[...] * gg_ref[0].astype(F32)).astype(BF16)


def _rnn(ux, gg, cw, cb, wa, ba, wx, bx, lam, h0, cbuf, pos0, tt):
    b, s, d = ux.shape
    tt = _row_tile(s, tt)
    pad = 8
    tile = pl.BlockSpec((1, tt, d), lambda i, t: (i, t, 0))
    per_b = lambda n: pl.BlockSpec((1, n, d), lambda i, t: (i, 0, 0))
    return pl.pallas_call(
        functools.partial(_rnn_body, pos0),
        grid=(b, s // tt),
        in_specs=[tile, tile, _resident((CONV_W, d)), _resident((1, d)), _resident((d, d)),
                  _resident((1, d)), _resident((d, d)), _resident((1, d)), _resident((1, d)),
                  per_b(1), per_b(CONV_W - 1)],
        out_specs=[tile, per_b(1), per_b(CONV_W - 1)],
        out_shape=[jax.ShapeDtypeStruct((b, s, d), BF16),
                   jax.ShapeDtypeStruct((b, 1, d), F32),
                   jax.ShapeDtypeStruct((b, CONV_W - 1, d), F32)],
        scratch_shapes=[pltpu.VMEM((tt + pad, d), F32), pltpu.VMEM((tt, d), F32),
                        pltpu.VMEM((tt, d), F32), pltpu.VMEM((tt, d), F32),
                        pltpu.VMEM((1, d), F32)],
        compiler_params=_params(2),
        name="rnn",
    )(ux, gg, cw, cb, wa, ba, wx, bx, lam, h0, cbuf)


MASKED_SCORE = -1e30
N_STREAMS = 4
ZERO_WEIGHT_AT = 104.0


def _softplus(z):
    return jnp.maximum(z, 0.0) + jnp.log(1.0 + jnp.exp(-jnp.abs(z)))


def _attn_body(q_pos0, n_past, q_ref, k_ref, v_ref, *rest):
    if n_past:
        kp_ref, vp_ref = rest[:2]
        rest = rest[2:]
    o_ref, bias_s, tri_s, z_s, sp_s, c_s, w_s, carry_s, acc_s = rest
    n_seq, sq, _ = q_ref.shape
    tq, tk = bias_s.shape[1:]
    nq_seq = sq // tq
    n_q = n_seq * nq_seq
    nh = HEADS_PER_STEP

    row = lax.broadcasted_iota(jnp.int32, (tk, tk), 0)
    col = lax.broadcasted_iota(jnp.int32, (tk, tk), 1)
    tri_s[...] = jnp.where(row >= col, 1.0, 0.0).astype(BF16)
    qrow = q_pos0 % tk + lax.broadcasted_iota(jnp.int32, (tq, tk), 0)
    kcol = lax.broadcasted_iota(jnp.int32, (tq, tk), 1)
    bias_s[0] = jnp.zeros((tq, tk), F32)
    bias_s[1] = jnp.where(kcol < qrow, 0.0, MASKED_SCORE)
    for ref in (z_s, sp_s, c_s, w_s, carry_s, acc_s, o_ref):
        ref[...] = jnp.zeros(ref.shape, ref.dtype)

    def place(qi):
        return qi // nq_seq, pl.multiple_of((qi % nq_seq) * tq, tq)

    def diag_tile(qi):
        return (q_pos0 + (qi % nq_seq) * tq) // tk

    def key_tile(qi, p):
        tile = jnp.maximum(diag_tile(qi) - p, 0)
        new = pl.ds(pl.multiple_of(jnp.maximum(tile - n_past, 0) * tk, tk), tk)
        old = pl.ds(pl.multiple_of(jnp.minimum(tile, max(n_past - 1, 0)) * tk, tk), tk)
        return qi // nq_seq, tile >= n_past, new, old

    def head(x, h):
        return x[:, h * HEAD_DIM:(h + 1) * HEAD_DIM]

    def nt_dot(a, b):
        return lax.dot_general(a, b, (((1,), (1,)), ((), ())), preferred_element_type=F32)

    def scores(q2, qi, p):
        seq, is_new, new, old = key_tile(qi, p)
        k2 = k_ref[seq, new, :]
        z = [nt_dot(head(q2, h), head(k2, h)) for h in range(nh)]
        if n_past:
            z = [jnp.where(is_new, z[h], _dot(head(q2, h), kp_ref[0, seq, h, :, old].astype(BF16)))
                 for h in range(nh)]
        return z

    def weighted_values(w, qi, p):
        seq, is_new, new, old = key_tile(qi, p)
        v2 = v_ref[seq, new, :]
        pv = [_dot(w[h], head(v2, h)) for h in range(nh)]
        if n_past:
            pv = [jnp.where(is_new, pv[h], nt_dot(w[h], vp_ref[0, seq, h, :, old].astype(BF16)))
                  for h in range(nh)]
        return pv

    def sub_step(u, streams, next_q):
        qi, p, real, spent = streams[u]

        seq, r0 = place(qi)
        pv = weighted_values([w_s[u, h] for h in range(nh)], qi, p)
        out = []
        for h in range(nh):
            a = jnp.where(p == 0, 0.0, acc_s[u, h]) + pv[h]
            acc_s[u, h] = a
            out.append(a)
        o_ref[seq, pl.ds(r0, tq), :] = jnp.where(
            real == 1, jnp.concatenate(out, axis=1).astype(BF16), o_ref[seq, pl.ds(r0, tq), :])

        t = (u + 1) % N_STREAMS
        qi_t, p_t, real_t, _ = streams[t]
        low = None
        for h in range(nh):
            c = c_s[t, h]
            base = jnp.where(p_t == 0, 0.0, carry_s[t, h])
            base = jnp.where(real_t == 1, base, -MASKED_SCORE)
            w_s[t, h] = jnp.exp(z_s[t, h] - c - base).astype(BF16)
            total = base + c[:, 0:1]
            carry_s[t, h] = total
            low = jnp.min(total) if low is None else jnp.minimum(low, jnp.min(total))
        streams[t] = (qi_t, p_t, real_t, (low > ZERO_WEIGHT_AT).astype(jnp.int32))

        t = (u + 2) % N_STREAMS
        for h in range(nh):
            c_s[t, h] = _dot(sp_s[t, h], tri_s[...])

        t = (u + 3) % N_STREAMS
        for h in range(nh):
            sp_s[t, h] = _softplus(z_s[t, h]).astype(BF16)

        done = jnp.logical_or(jnp.logical_or(real == 0, p >= diag_tile(qi)), spent == 1)
        take = jnp.logical_and(done, next_q < n_q)
        qi = jnp.where(take, next_q, qi)
        p = jnp.where(done, jnp.where(take, 0, 1), p + 1)
        real = jnp.where(done, take.astype(jnp.int32), 1)
        next_q = next_q + take.astype(jnp.int32)
        streams[u] = (qi, p, real, jnp.int32(0))

        seq, r0 = place(qi)
        q2 = q_ref[seq, pl.ds(r0, tq), :]
        bias = bias_s[jnp.where(p == 0, 1, 0)]
        z = scores(q2, qi, p)
        for h in range(nh):
            z_s[u, h] = bias + z[h]
        return streams, next_q

    def one_round(state):
        streams, next_q = list(state[0]), state[1]
        for u in range(N_STREAMS):
            streams, next_q = sub_step(u, streams, next_q)
        return tuple(streams), next_q

    def work_left(state):
        streams, next_q = state
        busy = next_q < n_q
        for _, _, real, _ in streams:
            busy = jnp.logical_or(busy, real == 1)
        return busy

    idle = (jnp.int32(0), jnp.int32(1), jnp.int32(0), jnp.int32(1))
    lax.while_loop(work_left, one_round, ((idle,) * N_STREAMS, jnp.int32(0)))


def _attn(q, k, v, tq, seqs_per_step, past=None):
    b, sq, d = q.shape
    sk = k.shape[1]
    tq = _row_tile(sq, tq)
    tk = min(sk, max(tq, 256))
    nb = seqs_per_step
    q_pos0 = 0 if past is None else past[0].shape[-1]
    n_past = q_pos0 // tk
    assert sk % tk == 0 and q_pos0 % tk == 0 and sq <= sk and (tq == tk or sq == tq)
    assert b % nb == 0
    n_col = d // LANES
    nh = HEADS_PER_STEP
    q_spec = pl.BlockSpec((nb, sq, LANES), lambda i, c: (i, 0, c))
    kv_spec = pl.BlockSpec((nb, sk, LANES), lambda i, c: (i, 0, c))
    in_specs, args = [q_spec, kv_spec, kv_spec], [q, k, v]
    if past is not None:
        layer = past[2]
        in_specs += [pl.BlockSpec((1, nb, nh, HEAD_DIM, q_pos0),
                                  lambda i, c: (layer, i, c, 0, 0))] * 2
        args += list(past[:2])
    ring = lambda width, dt: pltpu.VMEM((N_STREAMS, nh, tq, width), dt)
    return pl.pallas_call(
        functools.partial(_attn_body, q_pos0, n_past),
        grid=(b // nb, n_col),
        in_specs=in_specs,
        out_specs=q_spec,
        out_shape=jax.ShapeDtypeStruct((b, sq, d), BF16),
        scratch_shapes=[pltpu.VMEM((2, tq, tk), F32), pltpu.VMEM((tk, tk), BF16),
                        ring(tk, F32), ring(tk, BF16), ring(tk, F32), ring(tk, BF16),
                        ring(1, F32), ring(HEAD_DIM, F32)],
        compiler_params=_params(2),
        name="attn",
    )(*args)


def _merge_body(x_ref, y_ref, o_ref, sr_ref, sa_ref, wbr_ref, wba_ref, wo_ref, out_ref):
    m = (sr_ref[...].astype(F32) * _dot(y_ref[...], wbr_ref[...])
         + sa_ref[...].astype(F32) * _dot(o_ref[...], wba_ref[...]))
    out_ref[...] = x_ref[...] + _dot(m.astype(BF16), wo_ref[...])


def _merge(x, y, o, sr, sa, wbr, wba, wo, tm):
    rows, d = x.shape
    tm = _row_tile(rows, tm)
    return pl.pallas_call(
        _merge_body,
        grid=(rows // tm,),
        in_specs=[_row_spec(tm, d)] * 5 + [_resident((d, d))] * 3,
        out_specs=_row_spec(tm, d),
        out_shape=jax.ShapeDtypeStruct((rows, d), F32),
        compiler_params=_params(1),
        name="merge",
    )(x, y, o, sr, sa, wbr, wba, wo)


def _block_diag(w):
    n, c, _ = w.shape
    eye = jnp.eye(n, dtype=w.dtype)
    return (eye[:, None, :, None] * w[:, :, None, :]).reshape(n * c, n * c)


def _layer(x, p_all, lw, past, h0, cbuf, k_all, v_all, layer, depth, final, tm, tt, tq, seqs):
    b, s, d = x.shape
    rows = b * s
    x = x.reshape(rows, d)
    x = _ffn(x, lw["n_ffn1"], lw["f1_g"], lw["f1_u"], lw["f1_d"], tm)
    ux, gg, q, kb, vb, sr, sa, k_all, v_all = _inproj(x, lw["n_mix"], lw["w_in"], k_all, v_all,
                                                      layer, depth, b, s, tm)
    shp = (b, s, d)
    y, h_last, new_conv = _rnn(ux.reshape(shp), gg.reshape(shp), lw["conv_w"], lw["conv_b"],
                               lw["wa"], lw["ba"], lw["wx"], lw["bx"], lw["lam"], h0, cbuf,
                               0 if past is None else past[0].shape[-1], tt)
    kb, vb = kb.reshape(shp), vb.reshape(shp)
    if past is not None:
        fill = ((0, 0), (0, (-s) % 256), (0, 0))
        kb, vb = jnp.pad(kb, fill), jnp.pad(vb, fill)
    o = _attn(q.reshape(shp), kb, vb, tq, seqs, past)
    x = _merge(x, y.reshape(rows, d), o.reshape(rows, d), sr, sa,
               lw["w_br"], lw["w_ba"], lw["w_out"], tm)
    x = _ffn(x, lw["n_ffn2"], lw["f2_g"], lw["f2_u"], lw["f2_d"], tm,
             ple=(p_all.reshape(depth, rows, -1), layer, lw["n_ple"], lw["ple_g"], lw["ple_p"],
                  lw["final"], final))
    return x.reshape(shp), k_all, v_all, h_last.reshape(b, d), new_conv


def kernel(x_prompt, x_sample, p_prompt, p_sample, cache_k, cache_v, state_h, state_conv, norm_ffn1, ffn1_w_gate, ffn1_w_up, ffn1_w_down, norm_mix, w_in, conv_w, conv_b, lru_w_a, lru_b_a, lru_w_x, lru_b_x, lru_lambda, w_branch_rnn, w_branch_attn, w_out, norm_ffn2, ffn2_w_gate, ffn2_w_up, ffn2_w_down, norm_ple, ple_w_gate, ple_w_proj, final_norm):
    depth = w_in.shape[0]
    b, s, d = x_prompt.shape
    bd, sd, _ = x_sample.shape
    keys_minor = lambda a: jnp.transpose(a, (0, 1, 3, 4, 2))
    row = lambda a: a.reshape(1, -1)
    xp, xs = x_prompt, x_sample
    kp = vp = ks = vs = None
    outs = [[] for _ in range(4)]
    for i in range(depth):
        lw = dict(
            n_ffn1=row(norm_ffn1[i]), f1_g=ffn1_w_gate[i].astype(BF16),
            f1_u=ffn1_w_up[i].astype(BF16), f1_d=ffn1_w_down[i].astype(BF16),
            n_mix=row(norm_mix[i]), w_in=w_in[i].astype(BF16),
            conv_w=conv_w[i], conv_b=row(conv_b[i]),
            wa=_block_diag(lru_w_a[i]).astype(BF16), ba=row(lru_b_a[i]),
            wx=_block_diag(lru_w_x[i]).astype(BF16), bx=row(lru_b_x[i]),
            lam=row(lru_lambda[i]),
            w_br=w_branch_rnn[i].astype(BF16), w_ba=w_branch_attn[i].astype(BF16),
            w_out=w_out[i].astype(BF16),
            n_ffn2=row(norm_ffn2[i]), f2_g=ffn2_w_gate[i].astype(BF16),
            f2_u=ffn2_w_up[i].astype(BF16), f2_d=ffn2_w_down[i].astype(BF16),
            n_ple=row(norm_ple[i]), ple_g=ple_w_gate[i].astype(BF16),
            ple_p=ple_w_proj[i].astype(BF16), final=row(final_norm))
        final = i == depth - 1
        xp, kp, vp, hp, cp = _layer(
            xp, p_prompt, lw, None, jnp.zeros((b, 1, d), F32),
            jnp.zeros((b, CONV_W - 1, d), F32), kp, vp, i, depth, final,
            tm=512, tt=512, tq=256, seqs=1)
        xs, ks, vs, hs, cs = _layer(
            xs, p_sample, lw, (keys_minor(cache_k), keys_minor(cache_v), i), state_h[i].reshape(bd, 1, d),
            state_conv[i], ks, vs, i, depth, final, tm=bd * sd, tt=sd, tq=sd, seqs=bd)
        for lst, val in zip(outs, (hp, cp, hs, cs)):
            lst.append(val)
    hp, cp, hs, cs = (jnp.stack(lst) for lst in outs)
    seq_major = lambda a: jnp.transpose(a, (0, 1, 4, 2, 3))
    return (xp, xs, seq_major(kp), seq_major(vp), hp, cp, seq_major(ks), seq_major(vs), hs, cs)
```

```python
import functools

import jax
import jax.numpy as jnp
from jax import lax
from jax.experimental import pallas as pl
from jax.experimental.pallas import tpu as pltpu

N_HEADS = 16
HEAD_DIM = 64
N_LRU_BLOCKS = 16
LRU_C = 8.0
CONV_W = 4
RMS_EPS = 1e-6
N_IN_GROUPS = 7

LANES = 128
SUBLANES = 8
MXU_WIDTH = 256
HEADS_PER_STEP = LANES // HEAD_DIM
VMEM_LIMIT = 56 * 1024 * 1024

F32 = jnp.float32
BF16 = jnp.bfloat16


def _rms(x, g):
    return x * lax.rsqrt(jnp.mean(x * x, axis=-1, keepdims=True) + RMS_EPS) * g


def _dot(a, b):
    return jnp.dot(a, b, preferred_element_type=F32)


def _row_spec(tm, d):
    return pl.BlockSpec((tm, d), lambda i: (i, 0))


def _resident(layer, shape):
    return pl.BlockSpec((None,) + shape, lambda *_: (layer,) + (0,) * len(shape),
                        pipeline_mode=pl.Buffered(1))


def _sigmoid(x):
    return 0.5 * jnp.tanh(0.5 * x) + 0.5


def _params(n_axes):
    return pltpu.CompilerParams(dimension_semantics=("arbitrary",) * n_axes,
                                vmem_limit_bytes=VMEM_LIMIT)


def _row_tile(rows, want):
    tm = min(rows, want)
    assert rows % tm == 0
    return tm


def _ffn_body(with_ple, final, x_ref, g_ref, wg_ref, wu_ref, wd_ref, *rest):
    o_ref = rest[-1]
    x = x_ref[...]
    h = _rms(x, g_ref[...]).astype(BF16)
    gate = _dot(h, wg_ref[...])
    up = _dot(h, wu_ref[...])
    act = (gate * jax.nn.sigmoid(gate) * up).astype(BF16)
    x = x + 0.5 * _dot(act, wd_ref[...])
    if with_ple:
        p_ref, pg_ref, pwg_ref, pwp_ref, fg_ref = rest[:-1]
        gate = jax.nn.sigmoid(_dot(_rms(x, pg_ref[...]).astype(BF16), pwg_ref[...]))
        x = x + gate * _dot(p_ref[0].astype(BF16), pwp_ref[...])
        if final:
            x = _rms(x, fg_ref[...])
    o_ref[...] = x


def _ffn(x, layer, g, wg, wu, wd, tm, ple=None):
    rows, d = x.shape
    dff = wg.shape[-1]
    tm = _row_tile(rows, tm)
    in_specs = [_row_spec(tm, d), _resident(layer, (1, d)), _resident(layer, (d, dff)),
                _resident(layer, (d, dff)), _resident(layer, (dff, d))]
    args = [x, g, wg, wu, wd]
    final = False
    if ple is not None:
        p_all, pg, pwg, pwp, fg, final = ple
        dp = p_all.shape[-1]
        in_specs += [pl.BlockSpec((1, tm, dp), lambda i: (layer, i, 0)), _resident(layer, (1, d)),
                     _resident(layer, (d, d)), _resident(layer, (dp, d)), _resident(0, (1, d))]
        args += [p_all, pg, pwg, pwp, fg]
    return pl.pallas_call(
        functools.partial(_ffn_body, ple is not None, final),
        grid=(rows // tm,),
        in_specs=in_specs,
        out_specs=_row_spec(tm, d),
        out_shape=jax.ShapeDtypeStruct((rows, d), F32),
        compiler_params=_params(1),
        name="ffn",
    )(*args)


def _inproj_body(layer, x_ref, g_ref, w_ref, k_prev, v_prev, ux_ref, gg_ref, q_ref, kb_ref, vb_ref,
                 sr_ref, sa_ref, kt_ref, vt_ref):
    del k_prev, v_prev
    d = x_ref.shape[1]
    h = _rms(x_ref[...], g_ref[...]).astype(BF16)

    def col(i):
        return _dot(h, w_ref[:, i * d:(i + 1) * d])

    def put_transposed(ref, val):
        n_layers, n_seq, _, _, s = ref.shape
        mine = layer if n_layers > 1 else 0
        vt = val.T
        for b in range(n_seq):
            ref[mine, b] = vt[:, b * s:(b + 1) * s].reshape(N_HEADS, HEAD_DIM, s)
        for other in range(n_layers):
            if other != mine:
                ref[other] = jnp.zeros(ref.shape[1:], F32)

    ux_ref[...] = col(0)
    gg_ref[...] = jax.nn.gelu(col(1)).astype(BF16)
    q_ref[...] = (col(2) * (HEAD_DIM ** -0.5)).astype(BF16)
    k = col(3)
    put_transposed(kt_ref, k)
    kb_ref[...] = k.astype(BF16)
    v = col(4)
    put_transposed(vt_ref, v)
    vb_ref[...] = v.astype(BF16)
    sr_ref[...] = jax.nn.sigmoid(col(5)).astype(BF16)
    sa_ref[...] = jax.nn.sigmoid(col(6)).astype(BF16)


def _inproj(x, g, w_in, k_all, v_all, layer, depth, b, s, tm):
    rows, d = x.shape
    tm = _row_tile(rows, tm)
    assert tm % s == 0 or s % tm == 0
    n_seq, ts = max(tm // s, 1), min(tm, s)
    kv_shape = jax.ShapeDtypeStruct((depth, b, N_HEADS, HEAD_DIM, s), F32)
    dts = (F32, BF16, BF16, BF16, BF16, BF16, BF16)
    if k_all is None:
        k_all = v_all = jnp.zeros((1,), F32)
        aliases, n_layers, first = {}, depth, 0
    else:
        aliases, n_layers, first = {3: len(dts), 4: len(dts) + 1}, 1, layer
    kv_spec = pl.BlockSpec((n_layers, n_seq, N_HEADS, HEAD_DIM, ts),
                           lambda i: (first, i * tm // s, 0, 0, i % (s // ts)))
    return pl.pallas_call(
        functools.partial(_inproj_body, layer),
        grid=(rows // tm,),
        in_specs=[_row_spec(tm, d), _resident(layer, (1, d)),
                  _resident(layer, (d, N_IN_GROUPS * d)),
                  pl.BlockSpec(memory_space=pl.ANY), pl.BlockSpec(memory_space=pl.ANY)],
        out_specs=[_row_spec(tm, d)] * len(dts) + [kv_spec, kv_spec],
        out_shape=[jax.ShapeDtypeStruct((rows, d), t) for t in dts] + [kv_shape, kv_shape],
        input_output_aliases=aliases,
        compiler_params=_params(1),
        name="inproj",
    )(x, g, w_in, k_all, v_all)


def _rnn_body(pos0, ux_ref, gg_ref, cw_ref, cb_ref, wa_ref, ba_ref, wx_ref, bx_ref, lam_ref,
              h0_ref, cbuf_ref, y_ref, hl_ref, nc_ref, ubuf, a_s, b_s, h_s, hc):
    t_idx = pl.program_id(1)
    tt = ux_ref.shape[1]
    pad = ubuf.shape[0] - tt
    hist = CONV_W - 1

    @pl.when(t_idx == 0)
    def _():
        ubuf[pad - hist:pad, :] = cbuf_ref[0]
        hc[...] = h0_ref[0]

    @pl.when(t_idx != 0)
    def _():
        ubuf[0:pad, :] = ubuf[tt:tt + pad, :]

    u = ux_ref[0]
    ubuf[pad:pad + tt, :] = u
    first = lax.broadcasted_iota(jnp.int32, (SUBLANES, 1), 0) == 0

    def shift_down(x, row_before):
        rolled = pltpu.roll(x, 1, 0)
        return jnp.concatenate(
            [jnp.where(first, row_before, rolled[0:SUBLANES]), rolled[SUBLANES:]], axis=0)

    w = [cw_ref[j:j + 1, :] for j in range(CONV_W)]
    um1, um2, um3 = (ubuf[pad - j:pad - j + 1, :] for j in (1, 2, 3))
    acc = shift_down(u * w[0], um1 * w[0])
    acc = shift_down(u * w[1] + acc, um1 * w[1] + um2 * w[0])
    acc = shift_down(u * w[2] + acc, um1 * w[2] + (um2 * w[1] + um3 * w[0]))
    xc = cb_ref[...] + (u * w[3] + acc)

    xb = xc.astype(BF16)
    r = _sigmoid(_dot(xb, wa_ref[...]) + ba_ref[...])
    gate_i = _sigmoid(_dot(xb, wx_ref[...]) + bx_ref[...])
    lam = lam_ref[...]
    softplus_neg_lam = jnp.maximum(-lam, 0.0) + jnp.log1p(jnp.exp(-jnp.abs(lam)))
    log_a = (-LRU_C * softplus_neg_lam) * r
    a_s[...] = jnp.exp(log_a)
    th = jnp.tanh(log_a)
    sq = -2.0 * th / (1.0 - th)
    gated = gate_i * xc
    b_s[...] = jnp.where(sq > 0.0, sq * lax.rsqrt(sq), 0.0) * gated
    if pos0 == 0:
        @pl.when(t_idx == 0)
        def _():
            a_s[0:1, :] = jnp.zeros((1, a_s.shape[1]), F32)
            b_s[0:1, :] = gated[0:1, :]

    def step(t, h):
        h = a_s[pl.ds(t, 1), :] * h + b_s[pl.ds(t, 1), :]
        h_s[pl.ds(t, 1), :] = h
        return h

    h_end = lax.fori_loop(0, tt, step, hc[...], unroll=8)
    hc[...] = h_end
    hl_ref[0] = h_end
    nc_ref[0] = ubuf[pad + tt - hist:pad + tt, :]
    y_ref[0] = (h_s[...] * gg_ref[0].astype(F32)).astype(BF16)


def _rnn(ux, gg, layer, cw, cb, wa, ba, wx, bx, lam, h0, cbuf, pos0, tt):
    b, s, d = ux.shape
    tt = _row_tile(s, tt)
    pad = SUBLANES
    tile = pl.BlockSpec((1, tt, d), lambda i, t: (i, t, 0))
    per_b = lambda n: pl.BlockSpec((1, n, d), lambda i, t: (i, 0, 0))
    return pl.pallas_call(
        functools.partial(_rnn_body, pos0),
        grid=(b, s // tt),
        in_specs=[tile, tile, _resident(layer, (CONV_W, d)), _resident(layer, (1, d)),
                  _resident(layer, (d, d)), _resident(layer, (1, d)), _resident(layer, (d, d)),
                  _resident(layer, (1, d)), _resident(layer, (1, d)),
                  per_b(1), per_b(CONV_W - 1)],
        out_specs=[tile, per_b(1), per_b(CONV_W - 1)],
        out_shape=[jax.ShapeDtypeStruct((b, s, d), BF16),
                   jax.ShapeDtypeStruct((b, 1, d), F32),
                   jax.ShapeDtypeStruct((b, CONV_W - 1, d), F32)],
        scratch_shapes=[pltpu.VMEM((tt + pad, d), F32), pltpu.VMEM((tt, d), F32),
                        pltpu.VMEM((tt, d), F32), pltpu.VMEM((tt, d), F32),
                        pltpu.VMEM((1, d), F32)],
        compiler_params=_params(2),
        name="rnn",
    )(ux, gg, cw, cb, wa, ba, wx, bx, lam, h0, cbuf)


MASKED_SCORE = -1e30
N_STREAMS = 4
ZERO_WEIGHT_AT = 104.0


def _softplus(z):
    return jnp.maximum(z, 0.0) + jnp.log(1.0 + jnp.exp(-jnp.abs(z)))


def _attn_body(q_pos0, n_past, q_ref, k_ref, v_ref, *rest):
    if n_past:
        kp_ref, vp_ref = rest[:2]
        rest = rest[2:]
    o_ref, bias_s, tri_s, z_s, sp_s, c_s, w_s, carry_s, acc_s = rest
    n_seq, sq, _ = q_ref.shape
    tq, tk = bias_s.shape[1:]
    nq_seq = sq // tq
    n_q = n_seq * nq_seq
    nh = HEADS_PER_STEP

    row = lax.broadcasted_iota(jnp.int32, (tk, tk), 0)
    col = lax.broadcasted_iota(jnp.int32, (tk, tk), 1)
    tri_s[...] = jnp.where(row >= col, 1.0, 0.0).astype(BF16)
    qrow = q_pos0 % tk + lax.broadcasted_iota(jnp.int32, (tq, tk), 0)
    kcol = lax.broadcasted_iota(jnp.int32, (tq, tk), 1)
    bias_s[0] = jnp.zeros((tq, tk), F32)
    bias_s[1] = jnp.where(kcol < qrow, 0.0, MASKED_SCORE)
    for ref in (z_s, sp_s, c_s, w_s, carry_s, acc_s, o_ref):
        ref[...] = jnp.zeros(ref.shape, ref.dtype)

    def place(qi):
        return qi // nq_seq, pl.multiple_of((qi % nq_seq) * tq, tq)

    def diag_tile(qi):
        return (q_pos0 + (qi % nq_seq) * tq) // tk

    def key_tile(qi, p):
        tile = jnp.maximum(diag_tile(qi) - p, 0)
        new = pl.ds(pl.multiple_of(jnp.maximum(tile - n_past, 0) * tk, tk), tk)
        old = pl.ds(pl.multiple_of(jnp.minimum(tile, max(n_past - 1, 0)) * tk, tk), tk)
        return qi // nq_seq, tile >= n_past, new, old

    def head(x, h):
        return x[:, h * HEAD_DIM:(h + 1) * HEAD_DIM]

    def nt_dot(a, b):
        return lax.dot_general(a, b, (((1,), (1,)), ((), ())), preferred_element_type=F32)

    def scores(q2, qi, p):
        seq, is_new, new, old = key_tile(qi, p)
        k2 = k_ref[seq, new, :]
        z = [nt_dot(head(q2, h), head(k2, h)) for h in range(nh)]
        if n_past:
            z = [jnp.where(is_new, z[h], _dot(head(q2, h), kp_ref[0, seq, h, :, old].astype(BF16)))
                 for h in range(nh)]
        return z

    def weighted_values(w, qi, p):
        seq, is_new, new, old = key_tile(qi, p)
        v2 = v_ref[seq, new, :]
        pv = [_dot(w[h], head(v2, h)) for h in range(nh)]
        if n_past:
            pv = [jnp.where(is_new, pv[h], nt_dot(w[h], vp_ref[0, seq, h, :, old].astype(BF16)))
                  for h in range(nh)]
        return pv

    def sub_step(u, streams, next_q):
        qi, p, real, spent = streams[u]

        seq, r0 = place(qi)
        pv = weighted_values([w_s[u, h] for h in range(nh)], qi, p)
        out = []
        for h in range(nh):
            a = jnp.where(p == 0, 0.0, acc_s[u, h]) + pv[h]
            acc_s[u, h] = a
            out.append(a)
        o_ref[seq, pl.ds(r0, tq), :] = jnp.where(
            real == 1, jnp.concatenate(out, axis=1).astype(BF16), o_ref[seq, pl.ds(r0, tq), :])

        t = (u + 1) % N_STREAMS
        qi_t, p_t, real_t, _ = streams[t]
        low = None
        for h in range(nh):
            c = c_s[t, h]
            base = jnp.where(p_t == 0, 0.0, carry_s[t, h])
            base = jnp.where(real_t == 1, base, -MASKED_SCORE)
            w_s[t, h] = jnp.exp(z_s[t, h] - c - base).astype(BF16)
            total = base + c[:, 0:1]
            carry_s[t, h] = total
            low = jnp.min(total) if low is None else jnp.minimum(low, jnp.min(total))
        streams[t] = (qi_t, p_t, real_t, (low > ZERO_WEIGHT_AT).astype(jnp.int32))

        t = (u + 2) % N_STREAMS
        for h in range(nh):
            c_s[t, h] = _dot(sp_s[t, h], tri_s[...])

        t = (u + 3) % N_STREAMS
        for h in range(nh):
            sp_s[t, h] = _softplus(z_s[t, h]).astype(BF16)

        done = jnp.logical_or(jnp.logical_or(real == 0, p >= diag_tile(qi)), spent == 1)
        take = jnp.logical_and(done, next_q < n_q)
        qi = jnp.where(take, next_q, qi)
        p = jnp.where(done, jnp.where(take, 0, 1), p + 1)
        real = jnp.where(done, take.astype(jnp.int32), 1)
        next_q = next_q + take.astype(jnp.int32)
        streams[u] = (qi, p, real, jnp.int32(0))

        seq, r0 = place(qi)
        q2 = q_ref[seq, pl.ds(r0, tq), :]
        bias = bias_s[jnp.where(p == 0, 1, 0)]
        z = scores(q2, qi, p)
        for h in range(nh):
            z_s[u, h] = bias + z[h]
        return streams, next_q

    def one_round(state):
        streams, next_q = list(state[0]), state[1]
        for u in range(N_STREAMS):
            streams, next_q = sub_step(u, streams, next_q)
        return tuple(streams), next_q

    def work_left(state):
        streams, next_q = state
        busy = next_q < n_q
        for _, _, real, _ in streams:
            busy = jnp.logical_or(busy, real == 1)
        return busy

    idle = (jnp.int32(0), jnp.int32(1), jnp.int32(0), jnp.int32(1))
    lax.while_loop(work_left, one_round, ((idle,) * N_STREAMS, jnp.int32(0)))


def _attn(q, k, v, tq, seqs_per_step, past=None):
    b, sq, d = q.shape
    sk = k.shape[1]
    tq = _row_tile(sq, tq)
    tk = min(sk, max(tq, MXU_WIDTH))
    nb = seqs_per_step
    q_pos0 = 0 if past is None else past[0].shape[-1]
    n_past = q_pos0 // tk
    assert sk % tk == 0 and q_pos0 % tk == 0 and sq <= sk and (tq == tk or sq == tq)
    assert b % nb == 0
    n_col = d // LANES
    nh = HEADS_PER_STEP
    q_spec = pl.BlockSpec((nb, sq, LANES), lambda i, c: (i, 0, c))
    kv_spec = pl.BlockSpec((nb, sk, LANES), lambda i, c: (i, 0, c))
    in_specs, args = [q_spec, kv_spec, kv_spec], [q, k, v]
    if past is not None:
        layer = past[2]
        in_specs += [pl.BlockSpec((1, nb, nh, HEAD_DIM, q_pos0),
                                  lambda i, c: (layer, i, c, 0, 0))] * 2
        args += list(past[:2])
    ring = lambda width, dt: pltpu.VMEM((N_STREAMS, nh, tq, width), dt)
    return pl.pallas_call(
        functools.partial(_attn_body, q_pos0, n_past),
        grid=(b // nb, n_col),
        in_specs=in_specs,
        out_specs=q_spec,
        out_shape=jax.ShapeDtypeStruct((b, sq, d), BF16),
        scratch_shapes=[pltpu.VMEM((2, tq, tk), F32), pltpu.VMEM((tk, tk), BF16),
                        ring(tk, F32), ring(tk, BF16), ring(tk, F32), ring(tk, BF16),
                        ring(1, F32), ring(HEAD_DIM, F32)],
        compiler_params=_params(2),
        name="attn",
    )(*args)


def _merge_body(x_ref, y_ref, o_ref, sr_ref, sa_ref, wbr_ref, wba_ref, wo_ref, out_ref):
    m = (sr_ref[...].astype(F32) * _dot(y_ref[...], wbr_ref[...])
         + sa_ref[...].astype(F32) * _dot(o_ref[...], wba_ref[...]))
    out_ref[...] = x_ref[...] + _dot(m.astype(BF16), wo_ref[...])


def _merge(x, y, o, sr, sa, layer, wbr, wba, wo, tm):
    rows, d = x.shape
    tm = _row_tile(rows, tm)
    return pl.pallas_call(
        _merge_body,
        grid=(rows // tm,),
        in_specs=[_row_spec(tm, d)] * 5 + [_resident(layer, (d, d))] * 3,
        out_specs=_row_spec(tm, d),
        out_shape=jax.ShapeDtypeStruct((rows, d), F32),
        compiler_params=_params(1),
        name="merge",
    )(x, y, o, sr, sa, wbr, wba, wo)


def _block_diag(w):
    depth, n, c, _ = w.shape
    eye = jnp.eye(n, dtype=w.dtype)
    return (eye[:, None, :, None] * w[:, :, :, None, :]).reshape(depth, n * c, n * c)


def _layer(x, p_all, lw, past, h0, cbuf, k_all, v_all, layer, depth, final, tm, tt, tq, seqs):
    b, s, d = x.shape
    rows = b * s
    x = x.reshape(rows, d)
    x = _ffn(x, layer, lw["n_ffn1"], lw["f1_g"], lw["f1_u"], lw["f1_d"], tm)
    ux, gg, q, kb, vb, sr, sa, k_all, v_all = _inproj(x, lw["n_mix"], lw["w_in"], k_all, v_all,
                                                      layer, depth, b, s, tm)
    shp = (b, s, d)
    y, h_last, new_conv = _rnn(ux.reshape(shp), gg.reshape(shp), layer, lw["conv_w"], lw["conv_b"],
                               lw["wa"], lw["ba"], lw["wx"], lw["bx"], lw["lam"], h0, cbuf,
                               0 if past is None else past[0].shape[-1], tt)
    kb, vb = kb.reshape(shp), vb.reshape(shp)
    if past is not None:
        fill = ((0, 0), (0, (-s) % MXU_WIDTH), (0, 0))
        kb, vb = jnp.pad(kb, fill), jnp.pad(vb, fill)
    o = _attn(q.reshape(shp), kb, vb, tq, seqs, past)
    x = _merge(x, y.reshape(rows, d), o.reshape(rows, d), sr, sa, layer,
               lw["w_br"], lw["w_ba"], lw["w_out"], tm)
    x = _ffn(x, layer, lw["n_ffn2"], lw["f2_g"], lw["f2_u"], lw["f2_d"], tm,
             ple=(p_all.reshape(depth, rows, -1), lw["n_ple"], lw["ple_g"], lw["ple_p"], lw["final"],
                  final))
    return x.reshape(shp), k_all, v_all, h_last.reshape(b, d), new_conv


def kernel(x_prompt, x_sample, p_prompt, p_sample, cache_k, cache_v, state_h, state_conv, norm_ffn1, ffn1_w_gate, ffn1_w_up, ffn1_w_down, norm_mix, w_in, conv_w, conv_b, lru_w_a, lru_b_a, lru_w_x, lru_b_x, lru_lambda, w_branch_rnn, w_branch_attn, w_out, norm_ffn2, ffn2_w_gate, ffn2_w_up, ffn2_w_down, norm_ple, ple_w_gate, ple_w_proj, final_norm):
    depth = w_in.shape[0]
    b, s, d = x_prompt.shape
    bd, sd, _ = x_sample.shape
    keys_minor = lambda a: jnp.transpose(a, (0, 1, 3, 4, 2))
    row = lambda a: a.reshape(a.shape[0], 1, -1)
    cast = lambda a: a.astype(BF16)
    lw = dict(
        n_ffn1=row(norm_ffn1), f1_g=cast(ffn1_w_gate), f1_u=cast(ffn1_w_up), f1_d=cast(ffn1_w_down),
        n_mix=row(norm_mix), w_in=cast(w_in), conv_w=conv_w, conv_b=row(conv_b),
        wa=cast(_block_diag(lru_w_a)), ba=row(lru_b_a), wx=cast(_block_diag(lru_w_x)),
        bx=row(lru_b_x), lam=row(lru_lambda),
        w_br=cast(w_branch_rnn), w_ba=cast(w_branch_attn), w_out=cast(w_out),
        n_ffn2=row(norm_ffn2), f2_g=cast(ffn2_w_gate), f2_u=cast(ffn2_w_up), f2_d=cast(ffn2_w_down),
        n_ple=row(norm_ple), ple_g=cast(ple_w_gate), ple_p=cast(ple_w_proj),
        final=final_norm.reshape(1, 1, -1))
    xp, xs = x_prompt, x_sample
    kp = vp = ks = vs = None
    outs = [[] for _ in range(4)]
    for i in range(depth):
        final = i == depth - 1
        xp, kp, vp, hp, cp = _layer(
            xp, p_prompt, lw, None, jnp.zeros((b, 1, d), F32),
            jnp.zeros((b, CONV_W - 1, d), F32), kp, vp, i, depth, final,
            tm=512, tt=512, tq=256, seqs=1)
        xs, ks, vs, hs, cs = _layer(
            xs, p_sample, lw, (keys_minor(cache_k), keys_minor(cache_v), i), state_h[i].reshape(bd, 1, d),
            state_conv[i], ks, vs, i, depth, final, tm=bd * sd, tt=sd, tq=sd, seqs=bd)
        for lst, val in zip(outs, (hp, cp, hs, cs)):
            lst.append(val)
    hp, cp, hs, cs = (jnp.stack(lst) for lst in outs)
    seq_major = lambda a: jnp.transpose(a, (0, 1, 4, 2, 3))
    return (xp, xs, seq_major(kp), seq_major(vp), hp, cp, seq_major(ks), seq_major(vs), hs, cs)
```

```python
import functools

import jax
import jax.numpy as jnp
from jax import lax
from jax.experimental import pallas as pl
from jax.experimental.pallas import tpu as pltpu

N_HEADS = 16
HEAD_DIM = 64
N_LRU_BLOCKS = 16
LRU_C = 8.0
CONV_W = 4
RMS_EPS = 1e-6
N_IN_GROUPS = 7

LANES = 128
SUBLANES = 8
MXU_WIDTH = 256
HEADS_PER_STEP = LANES // HEAD_DIM
VMEM_LIMIT = 56 * 1024 * 1024

F32 = jnp.float32
BF16 = jnp.bfloat16


def _rms(x, g):
    return x * lax.rsqrt(jnp.mean(x * x, axis=-1, keepdims=True) + RMS_EPS) * g


def _dot(a, b):
    return jnp.dot(a, b, preferred_element_type=F32)


def _row_spec(tm, d):
    return pl.BlockSpec((tm, d), lambda i: (i, 0))


def _resident(layer, shape):
    return pl.BlockSpec((None,) + shape, lambda *_: (layer,) + (0,) * len(shape),
                        pipeline_mode=pl.Buffered(1))


def _sigmoid(x):
    return 0.5 * jnp.tanh(0.5 * x) + 0.5


def _params(n_axes):
    return pltpu.CompilerParams(dimension_semantics=("arbitrary",) * n_axes,
                                vmem_limit_bytes=VMEM_LIMIT)


def _row_tile(rows, want):
    tm = min(rows, want)
    assert rows % tm == 0
    return tm


def _ffn_body(with_ple, final, x_ref, g_ref, wg_ref, wu_ref, wd_ref, *rest):
    o_ref = rest[-1]
    x = x_ref[...]
    h = _rms(x, g_ref[...]).astype(BF16)
    gate = _dot(h, wg_ref[...])
    up = _dot(h, wu_ref[...])
    act = (gate * jax.nn.sigmoid(gate) * up).astype(BF16)
    x = x + 0.5 * _dot(act, wd_ref[...])
    if with_ple:
        p_ref, pg_ref, pwg_ref, pwp_ref, fg_ref = rest[:-1]
        gate = jax.nn.sigmoid(_dot(_rms(x, pg_ref[...]).astype(BF16), pwg_ref[...]))
        x = x + gate * _dot(p_ref[0].astype(BF16), pwp_ref[...])
        if final:
            x = _rms(x, fg_ref[...])
    o_ref[...] = x


def _ffn(x, layer, g, wg, wu, wd, tm, ple=None):
    rows, d = x.shape
    dff = wg.shape[-1]
    tm = _row_tile(rows, tm)
    in_specs = [_row_spec(tm, d), _resident(layer, (1, d)), _resident(layer, (d, dff)),
                _resident(layer, (d, dff)), _resident(layer, (dff, d))]
    args = [x, g, wg, wu, wd]
    final = False
    if ple is not None:
        p_all, pg, pwg, pwp, fg, final = ple
        dp = p_all.shape[-1]
        in_specs += [pl.BlockSpec((1, tm, dp), lambda i: (layer, i, 0)), _resident(layer, (1, d)),
                     _resident(layer, (d, d)), _resident(layer, (dp, d)), _resident(0, (1, d))]
        args += [p_all, pg, pwg, pwp, fg]
    return pl.pallas_call(
        functools.partial(_ffn_body, ple is not None, final),
        grid=(rows // tm,),
        in_specs=in_specs,
        out_specs=_row_spec(tm, d),
        out_shape=jax.ShapeDtypeStruct((rows, d), F32),
        compiler_params=_params(1),
        name="ffn",
    )(*args)


def _inproj_body(layer, x_ref, g_ref, w_ref, k_prev, v_prev, ux_ref, gg_ref, q_ref, kb_ref, vb_ref,
                 sr_ref, sa_ref, kt_ref, vt_ref):
    del k_prev, v_prev
    d = x_ref.shape[1]
    h = _rms(x_ref[...], g_ref[...]).astype(BF16)

    def col(i):
        return _dot(h, w_ref[:, i * d:(i + 1) * d])

    def put_transposed(ref, val):
        n_layers, n_seq, _, _, s = ref.shape
        mine = layer if n_layers > 1 else 0
        vt = val.T
        for b in range(n_seq):
            ref[mine, b] = vt[:, b * s:(b + 1) * s].reshape(N_HEADS, HEAD_DIM, s)
        for other in range(n_layers):
            if other != mine:
                ref[other] = jnp.zeros(ref.shape[1:], F32)

    ux_ref[...] = col(0)
    gg_ref[...] = jax.nn.gelu(col(1)).astype(BF16)
    q_ref[...] = (col(2) * (HEAD_DIM ** -0.5)).astype(BF16)
    k = col(3)
    put_transposed(kt_ref, k)
    kb_ref[...] = k.astype(BF16)
    v = col(4)
    put_transposed(vt_ref, v)
    vb_ref[...] = v.astype(BF16)
    sr_ref[...] = jax.nn.sigmoid(col(5)).astype(BF16)
    sa_ref[...] = jax.nn.sigmoid(col(6)).astype(BF16)


def _inproj(x, g, w_in, k_all, v_all, layer, depth, b, s, tm):
    rows, d = x.shape
    tm = _row_tile(rows, tm)
    assert tm % s == 0 or s % tm == 0
    n_seq, ts = max(tm // s, 1), min(tm, s)
    kv_shape = jax.ShapeDtypeStruct((depth, b, N_HEADS, HEAD_DIM, s), F32)
    dts = (F32, BF16, BF16, BF16, BF16, BF16, BF16)
    if k_all is None:
        k_all = v_all = jnp.zeros((1,), F32)
        aliases, n_layers, first = {}, depth, 0
    else:
        aliases, n_layers, first = {3: len(dts), 4: len(dts) + 1}, 1, layer
    kv_spec = pl.BlockSpec((n_layers, n_seq, N_HEADS, HEAD_DIM, ts),
                           lambda i: (first, i * tm // s, 0, 0, i % (s // ts)))
    return pl.pallas_call(
        functools.partial(_inproj_body, layer),
        grid=(rows // tm,),
        in_specs=[_row_spec(tm, d), _resident(layer, (1, d)),
                  _resident(layer, (d, N_IN_GROUPS * d)),
                  pl.BlockSpec(memory_space=pl.ANY), pl.BlockSpec(memory_space=pl.ANY)],
        out_specs=[_row_spec(tm, d)] * len(dts) + [kv_spec, kv_spec],
        out_shape=[jax.ShapeDtypeStruct((rows, d), t) for t in dts] + [kv_shape, kv_shape],
        input_output_aliases=aliases,
        compiler_params=_params(1),
        name="inproj",
    )(x, g, w_in, k_all, v_all)


def _rnn_body(pos0, ux_ref, gg_ref, cw_ref, cb_ref, wa_ref, ba_ref, wx_ref, bx_ref, lam_ref,
              h0_ref, cbuf_ref, y_ref, hl_ref, nc_ref, ubuf, a_s, b_s, h_s, hc):
    t_idx = pl.program_id(1)
    tt = ux_ref.shape[1]
    pad = ubuf.shape[0] - tt
    hist = CONV_W - 1

    @pl.when(t_idx == 0)
    def _():
        ubuf[pad - hist:pad, :] = cbuf_ref[0]
        hc[...] = h0_ref[0]

    @pl.when(t_idx != 0)
    def _():
        ubuf[0:pad, :] = ubuf[tt:tt + pad, :]

    u = ux_ref[0]
    ubuf[pad:pad + tt, :] = u
    first = lax.broadcasted_iota(jnp.int32, (SUBLANES, 1), 0) == 0

    def shift_down(x, row_before):
        rolled = pltpu.roll(x, 1, 0)
        return jnp.concatenate(
            [jnp.where(first, row_before, rolled[0:SUBLANES]), rolled[SUBLANES:]], axis=0)

    w = [cw_ref[j:j + 1, :] for j in range(CONV_W)]
    um1, um2, um3 = (ubuf[pad - j:pad - j + 1, :] for j in (1, 2, 3))
    acc = shift_down(u * w[0], um1 * w[0])
    acc = shift_down(u * w[1] + acc, um1 * w[1] + um2 * w[0])
    acc = shift_down(u * w[2] + acc, um1 * w[2] + (um2 * w[1] + um3 * w[0]))
    xc = cb_ref[...] + (u * w[3] + acc)

    xb = xc.astype(BF16)
    r = _sigmoid(_dot(xb, wa_ref[...]) + ba_ref[...])
    gate_i = _sigmoid(_dot(xb, wx_ref[...]) + bx_ref[...])
    lam = lam_ref[...]
    softplus_neg_lam = jnp.maximum(-lam, 0.0) + jnp.log1p(jnp.exp(-jnp.abs(lam)))
    log_a = (-LRU_C * softplus_neg_lam) * r
    a_s[...] = jnp.exp(log_a)
    th = jnp.tanh(log_a)
    sq = -2.0 * th / (1.0 - th)
    gated = gate_i * xc
    b_s[...] = jnp.where(sq > 0.0, sq * lax.rsqrt(sq), 0.0) * gated
    if pos0 == 0:
        @pl.when(t_idx == 0)
        def _():
            a_s[0:1, :] = jnp.zeros((1, a_s.shape[1]), F32)
            b_s[0:1, :] = gated[0:1, :]

    def step(t, h):
        h = a_s[pl.ds(t, 1), :] * h + b_s[pl.ds(t, 1), :]
        h_s[pl.ds(t, 1), :] = h
        return h

    h_end = lax.fori_loop(0, tt, step, hc[...], unroll=8)
    hc[...] = h_end
    hl_ref[0] = h_end
    nc_ref[0] = ubuf[pad + tt - hist:pad + tt, :]
    y_ref[0] = (h_s[...] * gg_ref[0].astype(F32)).astype(BF16)


def _rnn(ux, gg, layer, cw, cb, wa, ba, wx, bx, lam, h0, cbuf, pos0, tt):
    b, s, d = ux.shape
    tt = _row_tile(s, tt)
    pad = SUBLANES
    tile = pl.BlockSpec((1, tt, d), lambda i, t: (i, t, 0))
    per_b = lambda n: pl.BlockSpec((1, n, d), lambda i, t: (i, 0, 0))
    return pl.pallas_call(
        functools.partial(_rnn_body, pos0),
        grid=(b, s // tt),
        in_specs=[tile, tile, _resident(layer, (CONV_W, d)), _resident(layer, (1, d)),
                  _resident(layer, (d, d)), _resident(layer, (1, d)), _resident(layer, (d, d)),
                  _resident(layer, (1, d)), _resident(layer, (1, d)),
                  per_b(1), per_b(CONV_W - 1)],
        out_specs=[tile, per_b(1), per_b(CONV_W - 1)],
        out_shape=[jax.ShapeDtypeStruct((b, s, d), BF16),
                   jax.ShapeDtypeStruct((b, 1, d), F32),
                   jax.ShapeDtypeStruct((b, CONV_W - 1, d), F32)],
        scratch_shapes=[pltpu.VMEM((tt + pad, d), F32), pltpu.VMEM((tt, d), F32),
                        pltpu.VMEM((tt, d), F32), pltpu.VMEM((tt, d), F32),
                        pltpu.VMEM((1, d), F32)],
        compiler_params=_params(2),
        name="rnn",
    )(ux, gg, cw, cb, wa, ba, wx, bx, lam, h0, cbuf)


MASKED_SCORE = -1e30
N_STREAMS = 4
ZERO_WEIGHT_AT = 104.0


def _softplus(z):
    return jnp.maximum(z, 0.0) + jnp.log(1.0 + jnp.exp(-jnp.abs(z)))


def _attn_body(q_pos0, n_past, q_ref, k_ref, v_ref, *rest):
    if n_past:
        kp_ref, vp_ref = rest[:2]
        rest = rest[2:]
    o_ref, bias_s, tri_s, z_s, sp_s, c_s, w_s, carry_s, acc_s = rest
    n_seq, sq, _ = q_ref.shape
    tq, tk = bias_s.shape[1:]
    nq_seq = sq // tq
    n_q = n_seq * nq_seq
    nh = HEADS_PER_STEP

    row = lax.broadcasted_iota(jnp.int32, (tk, tk), 0)
    col = lax.broadcasted_iota(jnp.int32, (tk, tk), 1)
    tri_s[...] = jnp.where(row > col, 1.0, 0.0).astype(BF16)
    qrow = q_pos0 % tk + lax.broadcasted_iota(jnp.int32, (tq, tk), 0)
    kcol = lax.broadcasted_iota(jnp.int32, (tq, tk), 1)
    bias_s[0] = jnp.zeros((tq, tk), F32)
    bias_s[1] = jnp.where(kcol < qrow, 0.0, MASKED_SCORE)
    for ref in (z_s, sp_s, c_s, w_s, carry_s, acc_s, o_ref):
        ref[...] = jnp.zeros(ref.shape, ref.dtype)

    def place(qi):
        return qi // nq_seq, pl.multiple_of((qi % nq_seq) * tq, tq)

    def diag_tile(qi):
        return (q_pos0 + (qi % nq_seq) * tq) // tk

    def key_tile(qi, p):
        tile = jnp.maximum(diag_tile(qi) - p, 0)
        new = pl.ds(pl.multiple_of(jnp.maximum(tile - n_past, 0) * tk, tk), tk)
        old = pl.ds(pl.multiple_of(jnp.minimum(tile, max(n_past - 1, 0)) * tk, tk), tk)
        return qi // nq_seq, tile >= n_past, new, old

    def head(x, h):
        return x[:, h * HEAD_DIM:(h + 1) * HEAD_DIM]

    def nt_dot(a, b):
        return lax.dot_general(a, b, (((1,), (1,)), ((), ())), preferred_element_type=F32)

    def scores(q2, qi, p):
        seq, is_new, new, old = key_tile(qi, p)
        k2 = k_ref[seq, new, :]
        z = [nt_dot(head(q2, h), head(k2, h)) for h in range(nh)]
        if n_past:
            z = [jnp.where(is_new, z[h], _dot(head(q2, h), kp_ref[0, seq, h, :, old].astype(BF16)))
                 for h in range(nh)]
        return z

    def weighted_values(w, qi, p):
        seq, is_new, new, old = key_tile(qi, p)
        v2 = v_ref[seq, new, :]
        pv = [_dot(w[h], head(v2, h)) for h in range(nh)]
        if n_past:
            pv = [jnp.where(is_new, pv[h], nt_dot(w[h], vp_ref[0, seq, h, :, old].astype(BF16)))
                  for h in range(nh)]
        return pv

    def sub_step(u, streams, next_q):
        qi, p, real, spent = streams[u]

        seq, r0 = place(qi)
        pv = weighted_values([w_s[u, h] for h in range(nh)], qi, p)
        out = []
        for h in range(nh):
            a = jnp.where(p == 0, 0.0, acc_s[u, h]) + pv[h]
            acc_s[u, h] = a
            out.append(a)
        o_ref[seq, pl.ds(r0, tq), :] = jnp.where(
            real == 1, jnp.concatenate(out, axis=1).astype(BF16), o_ref[seq, pl.ds(r0, tq), :])

        t = (u + 1) % N_STREAMS
        qi_t, p_t, real_t, _ = streams[t]
        low = None
        for h in range(nh):
            c = c_s[t, h]
            base = jnp.where(p_t == 0, 0.0, carry_s[t, h])
            base = jnp.where(real_t == 1, base, -MASKED_SCORE)
            w_s[t, h] = jnp.exp(z_s[t, h] - c - base).astype(BF16)
            total = base + (c[:, 0:1] + sp_s[t, h, :, 0:1].astype(F32))
            carry_s[t, h] = total
            low = jnp.min(total) if low is None else jnp.minimum(low, jnp.min(total))
        streams[t] = (qi_t, p_t, real_t, (low > ZERO_WEIGHT_AT).astype(jnp.int32))

        t = (u + 2) % N_STREAMS
        for h in range(nh):
            c_s[t, h] = _dot(sp_s[t, h], tri_s[...])

        t = (u + 3) % N_STREAMS
        for h in range(nh):
            z = z_s[t, h]
            sp = _softplus(z)
            sp_s[t, h] = sp.astype(BF16)
            z_s[t, h] = z - sp

        done = jnp.logical_or(jnp.logical_or(real == 0, p >= diag_tile(qi)), spent == 1)
        take = jnp.logical_and(done, next_q < n_q)
        qi = jnp.where(take, next_q, qi)
        p = jnp.where(done, jnp.where(take, 0, 1), p + 1)
        real = jnp.where(done, take.astype(jnp.int32), 1)
        next_q = next_q + take.astype(jnp.int32)
        streams[u] = (qi, p, real, jnp.int32(0))

        seq, r0 = place(qi)
        q2 = q_ref[seq, pl.ds(r0, tq), :]
        bias = bias_s[jnp.where(p == 0, 1, 0)]
        z = scores(q2, qi, p)
        for h in range(nh):
            z_s[u, h] = bias + z[h]
        return streams, next_q

    def one_round(state):
        streams, next_q = list(state[0]), state[1]
        for u in range(N_STREAMS):
            streams, next_q = sub_step(u, streams, next_q)
        return tuple(streams), next_q

    def work_left(state):
        streams, next_q = state
        busy = next_q < n_q
        for _, _, real, _ in streams:
            busy = jnp.logical_or(busy, real == 1)
        return busy

    idle = (jnp.int32(0), jnp.int32(1), jnp.int32(0), jnp.int32(1))
    lax.while_loop(work_left, one_round, ((idle,) * N_STREAMS, jnp.int32(0)))


def _attn(q, k, v, tq, seqs_per_step, past=None):
    b, sq, d = q.shape
    sk = k.shape[1]
    tq = _row_tile(sq, tq)
    tk = min(sk, max(tq, MXU_WIDTH))
    nb = seqs_per_step
    q_pos0 = 0 if past is None else past[0].shape[-1]
    n_past = q_pos0 // tk
    assert sk % tk == 0 and q_pos0 % tk == 0 and sq <= sk and (tq == tk or sq == tq)
    assert b % nb == 0
    n_col = d // LANES
    nh = HEADS_PER_STEP
    q_spec = pl.BlockSpec((nb, sq, LANES), lambda i, c: (i, 0, c))
    kv_spec = pl.BlockSpec((nb, sk, LANES), lambda i, c: (i, 0, c))
    in_specs, args = [q_spec, kv_spec, kv_spec], [q, k, v]
    if past is not None:
        layer = past[2]
        in_specs += [pl.BlockSpec((1, nb, nh, HEAD_DIM, q_pos0),
                                  lambda i, c: (layer, i, c, 0, 0))] * 2
        args += list(past[:2])
    ring = lambda width, dt: pltpu.VMEM((N_STREAMS, nh, tq, width), dt)
    return pl.pallas_call(
        functools.partial(_attn_body, q_pos0, n_past),
        grid=(b // nb, n_col),
        in_specs=in_specs,
        out_specs=q_spec,
        out_shape=jax.ShapeDtypeStruct((b, sq, d), BF16),
        scratch_shapes=[pltpu.VMEM((2, tq, tk), F32), pltpu.VMEM((tk, tk), BF16),
                        ring(tk, F32), ring(tk, BF16), ring(tk, F32), ring(tk, BF16),
                        ring(1, F32), ring(HEAD_DIM, F32)],
        compiler_params=_params(2),
        name="attn",
    )(*args)


def _merge_body(x_ref, y_ref, o_ref, sr_ref, sa_ref, wbr_ref, wba_ref, wo_ref, out_ref):
    m = (sr_ref[...].astype(F32) * _dot(y_ref[...], wbr_ref[...])
         + sa_ref[...].astype(F32) * _dot(o_ref[...], wba_ref[...]))
    out_ref[...] = x_ref[...] + _dot(m.astype(BF16), wo_ref[...])


def _merge(x, y, o, sr, sa, layer, wbr, wba, wo, tm):
    rows, d = x.shape
    tm = _row_tile(rows, tm)
    return pl.pallas_call(
        _merge_body,
        grid=(rows // tm,),
        in_specs=[_row_spec(tm, d)] * 5 + [_resident(layer, (d, d))] * 3,
        out_specs=_row_spec(tm, d),
        out_shape=jax.ShapeDtypeStruct((rows, d), F32),
        compiler_params=_params(1),
        name="merge",
    )(x, y, o, sr, sa, wbr, wba, wo)


def _block_diag(w):
    depth, n, c, _ = w.shape
    eye = jnp.eye(n, dtype=w.dtype)
    return (eye[:, None, :, None] * w[:, :, :, None, :]).reshape(depth, n * c, n * c)


def _layer(x, p_all, lw, past, h0, cbuf, k_all, v_all, layer, depth, final, tm, tt, tq, seqs):
    b, s, d = x.shape
    rows = b * s
    x = x.reshape(rows, d)
    x = _ffn(x, layer, lw["n_ffn1"], lw["f1_g"], lw["f1_u"], lw["f1_d"], tm)
    ux, gg, q, kb, vb, sr, sa, k_all, v_all = _inproj(x, lw["n_mix"], lw["w_in"], k_all, v_all,
                                                      layer, depth, b, s, tm)
    shp = (b, s, d)
    y, h_last, new_conv = _rnn(ux.reshape(shp), gg.reshape(shp), layer, lw["conv_w"], lw["conv_b"],
                               lw["wa"], lw["ba"], lw["wx"], lw["bx"], lw["lam"], h0, cbuf,
                               0 if past is None else past[0].shape[-1], tt)
    kb, vb = kb.reshape(shp), vb.reshape(shp)
    if past is not None:
        fill = ((0, 0), (0, (-s) % MXU_WIDTH), (0, 0))
        kb, vb = jnp.pad(kb, fill), jnp.pad(vb, fill)
    o = _attn(q.reshape(shp), kb, vb, tq, seqs, past)
    x = _merge(x, y.reshape(rows, d), o.reshape(rows, d), sr, sa, layer,
               lw["w_br"], lw["w_ba"], lw["w_out"], tm)
    x = _ffn(x, layer, lw["n_ffn2"], lw["f2_g"], lw["f2_u"], lw["f2_d"], tm,
             ple=(p_all.reshape(depth, rows, -1), lw["n_ple"], lw["ple_g"], lw["ple_p"], lw["final"],
                  final))
    return x.reshape(shp), k_all, v_all, h_last.reshape(b, d), new_conv


def kernel(x_prompt, x_sample, p_prompt, p_sample, cache_k, cache_v, state_h, state_conv, norm_ffn1, ffn1_w_gate, ffn1_w_up, ffn1_w_down, norm_mix, w_in, conv_w, conv_b, lru_w_a, lru_b_a, lru_w_x, lru_b_x, lru_lambda, w_branch_rnn, w_branch_attn, w_out, norm_ffn2, ffn2_w_gate, ffn2_w_up, ffn2_w_down, norm_ple, ple_w_gate, ple_w_proj, final_norm):
    depth = w_in.shape[0]
    b, s, d = x_prompt.shape
    bd, sd, _ = x_sample.shape
    keys_minor = lambda a: jnp.transpose(a, (0, 1, 3, 4, 2))
    row = lambda a: a.reshape(a.shape[0], 1, -1)
    cast = lambda a: a.astype(BF16)
    lw = dict(
        n_ffn1=row(norm_ffn1), f1_g=cast(ffn1_w_gate), f1_u=cast(ffn1_w_up), f1_d=cast(ffn1_w_down),
        n_mix=row(norm_mix), w_in=cast(w_in), conv_w=conv_w, conv_b=row(conv_b),
        wa=cast(_block_diag(lru_w_a)), ba=row(lru_b_a), wx=cast(_block_diag(lru_w_x)),
        bx=row(lru_b_x), lam=row(lru_lambda),
        w_br=cast(w_branch_rnn), w_ba=cast(w_branch_attn), w_out=cast(w_out),
        n_ffn2=row(norm_ffn2), f2_g=cast(ffn2_w_gate), f2_u=cast(ffn2_w_up), f2_d=cast(ffn2_w_down),
        n_ple=row(norm_ple), ple_g=cast(ple_w_gate), ple_p=cast(ple_w_proj),
        final=final_norm.reshape(1, 1, -1))
    xp, xs = x_prompt, x_sample
    kp = vp = ks = vs = None
    outs = [[] for _ in range(4)]
    for i in range(depth):
        final = i == depth - 1
        xp, kp, vp, hp, cp = _layer(
            xp, p_prompt, lw, None, jnp.zeros((b, 1, d), F32),
            jnp.zeros((b, CONV_W - 1, d), F32), kp, vp, i, depth, final,
            tm=512, tt=512, tq=256, seqs=1)
        xs, ks, vs, hs, cs = _layer(
            xs, p_sample, lw, (keys_minor(cache_k), keys_minor(cache_v), i), state_h[i].reshape(bd, 1, d),
            state_conv[i], ks, vs, i, depth, final, tm=bd * sd, tt=sd, tq=sd, seqs=bd)
        for lst, val in zip(outs, (hp, cp, hs, cs)):
            lst.append(val)
    hp, cp, hs, cs = (jnp.stack(lst) for lst in outs)
    seq_major = lambda a: jnp.transpose(a, (0, 1, 4, 2, 3))
    return (xp, xs, seq_major(kp), seq_major(vp), hp, cp, seq_major(ks), seq_major(vs), hs, cs)
```

```python
import functools

import jax
import jax.numpy as jnp
from jax import lax
from jax.experimental import pallas as pl
from jax.experimental.pallas import tpu as pltpu

N_HEADS = 16
HEAD_DIM = 64
N_LRU_BLOCKS = 16
LRU_C = 8.0
CONV_W = 4
RMS_EPS = 1e-6
N_IN_GROUPS = 7

LANES = 128
SUBLANES = 8
MXU_WIDTH = 256
HEADS_PER_STEP = LANES // HEAD_DIM
VMEM_LIMIT = 56 * 1024 * 1024

F32 = jnp.float32
BF16 = jnp.bfloat16


def _rms(x, g):
    return x * lax.rsqrt(jnp.mean(x * x, axis=-1, keepdims=True) + RMS_EPS) * g


def _dot(a, b):
    return jnp.dot(a, b, preferred_element_type=F32)


def _row_spec(tm, d):
    return pl.BlockSpec((tm, d), lambda i: (i, 0))


def _resident(layer, shape):
    return pl.BlockSpec((None,) + shape, lambda *_: (layer,) + (0,) * len(shape),
                        pipeline_mode=pl.Buffered(1))


def _sigmoid(x):
    return 0.5 * jnp.tanh(0.5 * x) + 0.5


def _params(n_axes):
    return pltpu.CompilerParams(dimension_semantics=("arbitrary",) * n_axes,
                                vmem_limit_bytes=VMEM_LIMIT)


def _row_tile(rows, want):
    tm = min(rows, want)
    assert rows % tm == 0
    return tm


def _ffn_body(with_ple, final, x_ref, g_ref, wg_ref, wu_ref, wd_ref, *rest):
    o_ref = rest[-1]
    x = x_ref[...]
    h = _rms(x, g_ref[...]).astype(BF16)
    gate = _dot(h, wg_ref[...])
    up = _dot(h, wu_ref[...])
    act = (gate * jax.nn.sigmoid(gate) * up).astype(BF16)
    x = x + 0.5 * _dot(act, wd_ref[...])
    if with_ple:
        p_ref, pg_ref, pwg_ref, pwp_ref, fg_ref = rest[:-1]
        gate = jax.nn.sigmoid(_dot(_rms(x, pg_ref[...]).astype(BF16), pwg_ref[...]))
        x = x + gate * _dot(p_ref[0].astype(BF16), pwp_ref[...])
        if final:
            x = _rms(x, fg_ref[...])
    o_ref[...] = x


def _ffn(x, layer, g, wg, wu, wd, tm, ple=None):
    rows, d = x.shape
    dff = wg.shape[-1]
    tm = _row_tile(rows, tm)
    in_specs = [_row_spec(tm, d), _resident(layer, (1, d)), _resident(layer, (d, dff)),
                _resident(layer, (d, dff)), _resident(layer, (dff, d))]
    args = [x, g, wg, wu, wd]
    final = False
    if ple is not None:
        p_all, pg, pwg, pwp, fg, final = ple
        dp = p_all.shape[-1]
        in_specs += [pl.BlockSpec((1, tm, dp), lambda i: (layer, i, 0)), _resident(layer, (1, d)),
                     _resident(layer, (d, d)), _resident(layer, (dp, d)), _resident(0, (1, d))]
        args += [p_all, pg, pwg, pwp, fg]
    return pl.pallas_call(
        functools.partial(_ffn_body, ple is not None, final),
        grid=(rows // tm,),
        in_specs=in_specs,
        out_specs=_row_spec(tm, d),
        out_shape=jax.ShapeDtypeStruct((rows, d), F32),
        compiler_params=_params(1),
        name="ffn",
    )(*args)


def _inproj_body(layer, x_ref, g_ref, w_ref, k_prev, v_prev, ux_ref, gg_ref, q_ref, kb_ref, vb_ref,
                 sr_ref, sa_ref, kt_ref, vt_ref):
    del k_prev, v_prev
    d = x_ref.shape[1]
    h = _rms(x_ref[...], g_ref[...]).astype(BF16)

    def col(i):
        return _dot(h, w_ref[:, i * d:(i + 1) * d])

    def put_transposed(ref, val):
        n_layers, n_seq, _, _, s = ref.shape
        mine = layer if n_layers > 1 else 0
        vt = val.T
        for b in range(n_seq):
            ref[mine, b] = vt[:, b * s:(b + 1) * s].reshape(N_HEADS, HEAD_DIM, s)
        for other in range(n_layers):
            if other != mine:
                ref[other] = jnp.zeros(ref.shape[1:], F32)

    sr_ref[...] = jax.nn.sigmoid(col(5)).astype(BF16)
    sa_ref[...] = jax.nn.sigmoid(col(6)).astype(BF16)
    gg_ref[...] = jax.nn.gelu(col(1)).astype(BF16)
    k = col(3)
    put_transposed(kt_ref, k)
    kb_ref[...] = k.astype(BF16)
    v = col(4)
    put_transposed(vt_ref, v)
    vb_ref[...] = v.astype(BF16)
    q_ref[...] = (col(2) * (HEAD_DIM ** -0.5)).astype(BF16)
    ux_ref[...] = col(0)


def _inproj(x, g, w_in, k_all, v_all, layer, depth, b, s, tm):
    rows, d = x.shape
    tm = _row_tile(rows, tm)
    assert tm % s == 0 or s % tm == 0
    n_seq, ts = max(tm // s, 1), min(tm, s)
    kv_shape = jax.ShapeDtypeStruct((depth, b, N_HEADS, HEAD_DIM, s), F32)
    dts = (F32, BF16, BF16, BF16, BF16, BF16, BF16)
    if k_all is None:
        k_all = v_all = jnp.zeros((1,), F32)
        aliases, n_layers, first = {}, depth, 0
    else:
        aliases, n_layers, first = {3: len(dts), 4: len(dts) + 1}, 1, layer
    kv_spec = pl.BlockSpec((n_layers, n_seq, N_HEADS, HEAD_DIM, ts),
                           lambda i: (first, i * tm // s, 0, 0, i % (s // ts)))
    return pl.pallas_call(
        functools.partial(_inproj_body, layer),
        grid=(rows // tm,),
        in_specs=[_row_spec(tm, d), _resident(layer, (1, d)),
                  _resident(layer, (d, N_IN_GROUPS * d)),
                  pl.BlockSpec(memory_space=pl.ANY), pl.BlockSpec(memory_space=pl.ANY)],
        out_specs=[_row_spec(tm, d)] * len(dts) + [kv_spec, kv_spec],
        out_shape=[jax.ShapeDtypeStruct((rows, d), t) for t in dts] + [kv_shape, kv_shape],
        input_output_aliases=aliases,
        compiler_params=_params(1),
        name="inproj",
    )(x, g, w_in, k_all, v_all)


def _rnn_body(pos0, ux_ref, gg_ref, cw_ref, cb_ref, wa_ref, ba_ref, wx_ref, bx_ref, lam_ref,
              h0_ref, cbuf_ref, y_ref, hl_ref, nc_ref, ubuf, a_s, b_s, h_s, hc):
    t_idx = pl.program_id(1)
    tt = ux_ref.shape[1]
    pad = ubuf.shape[0] - tt
    hist = CONV_W - 1

    @pl.when(t_idx == 0)
    def _():
        ubuf[pad - hist:pad, :] = cbuf_ref[0]
        hc[...] = h0_ref[0]

    @pl.when(t_idx != 0)
    def _():
        ubuf[0:pad, :] = ubuf[tt:tt + pad, :]

    u = ux_ref[0]
    ubuf[pad:pad + tt, :] = u
    first = lax.broadcasted_iota(jnp.int32, (SUBLANES, 1), 0) == 0

    def shift_down(x, row_before):
        rolled = pltpu.roll(x, 1, 0)
        return jnp.concatenate(
            [jnp.where(first, row_before, rolled[0:SUBLANES]), rolled[SUBLANES:]], axis=0)

    w = [cw_ref[j:j + 1, :] for j in range(CONV_W)]
    um1, um2, um3 = (ubuf[pad - j:pad - j + 1, :] for j in (1, 2, 3))
    acc = shift_down(u * w[0], um1 * w[0])
    acc = shift_down(u * w[1] + acc, um1 * w[1] + um2 * w[0])
    acc = shift_down(u * w[2] + acc, um1 * w[2] + (um2 * w[1] + um3 * w[0]))
    xc = cb_ref[...] + (u * w[3] + acc)

    xb = xc.astype(BF16)
    r = _sigmoid(_dot(xb, wa_ref[...]) + ba_ref[...])
    gate_i = _sigmoid(_dot(xb, wx_ref[...]) + bx_ref[...])
    lam = lam_ref[...]
    softplus_neg_lam = jnp.maximum(-lam, 0.0) + jnp.log1p(jnp.exp(-jnp.abs(lam)))
    log_a = (-LRU_C * softplus_neg_lam) * r
    a_s[...] = jnp.exp(log_a)
    th = jnp.tanh(log_a)
    sq = -2.0 * th / (1.0 - th)
    gated = gate_i * xc
    b_s[...] = jnp.where(sq > 0.0, sq * lax.rsqrt(sq), 0.0) * gated
    if pos0 == 0:
        @pl.when(t_idx == 0)
        def _():
            a_s[0:1, :] = jnp.zeros((1, a_s.shape[1]), F32)
            b_s[0:1, :] = gated[0:1, :]

    def step(t, h):
        h = a_s[pl.ds(t, 1), :] * h + b_s[pl.ds(t, 1), :]
        h_s[pl.ds(t, 1), :] = h
        return h

    h_end = lax.fori_loop(0, tt, step, hc[...], unroll=8)
    hc[...] = h_end
    hl_ref[0] = h_end
    nc_ref[0] = ubuf[pad + tt - hist:pad + tt, :]
    y_ref[0] = (h_s[...] * gg_ref[0].astype(F32)).astype(BF16)


def _rnn(ux, gg, layer, cw, cb, wa, ba, wx, bx, lam, h0, cbuf, pos0, tt):
    b, s, d = ux.shape
    tt = _row_tile(s, tt)
    pad = SUBLANES
    tile = pl.BlockSpec((1, tt, d), lambda i, t: (i, t, 0))
    per_b = lambda n: pl.BlockSpec((1, n, d), lambda i, t: (i, 0, 0))
    return pl.pallas_call(
        functools.partial(_rnn_body, pos0),
        grid=(b, s // tt),
        in_specs=[tile, tile, _resident(layer, (CONV_W, d)), _resident(layer, (1, d)),
                  _resident(layer, (d, d)), _resident(layer, (1, d)), _resident(layer, (d, d)),
                  _resident(layer, (1, d)), _resident(layer, (1, d)),
                  per_b(1), per_b(CONV_W - 1)],
        out_specs=[tile, per_b(1), per_b(CONV_W - 1)],
        out_shape=[jax.ShapeDtypeStruct((b, s, d), BF16),
                   jax.ShapeDtypeStruct((b, 1, d), F32),
                   jax.ShapeDtypeStruct((b, CONV_W - 1, d), F32)],
        scratch_shapes=[pltpu.VMEM((tt + pad, d), F32), pltpu.VMEM((tt, d), F32),
                        pltpu.VMEM((tt, d), F32), pltpu.VMEM((tt, d), F32),
                        pltpu.VMEM((1, d), F32)],
        compiler_params=_params(2),
        name="rnn",
    )(ux, gg, cw, cb, wa, ba, wx, bx, lam, h0, cbuf)


MASKED_SCORE = -1e30
N_STREAMS = 4
ZERO_WEIGHT_AT = 104.0


def _softplus(z):
    return jnp.maximum(z, 0.0) + jnp.log(1.0 + jnp.exp(-jnp.abs(z)))


def _attn_body(q_pos0, n_past, q_ref, k_ref, v_ref, *rest):
    if n_past:
        kp_ref, vp_ref = rest[:2]
        rest = rest[2:]
    o_ref, bias_s, tri_s, z_s, sp_s, c_s, w_s, carry_s, acc_s = rest
    n_seq, sq, _ = q_ref.shape
    tq, tk = bias_s.shape[1:]
    nq_seq = sq // tq
    n_q = n_seq * nq_seq
    nh = HEADS_PER_STEP

    row = lax.broadcasted_iota(jnp.int32, (tk, tk), 0)
    col = lax.broadcasted_iota(jnp.int32, (tk, tk), 1)
    tri_s[...] = jnp.where(row > col, 1.0, 0.0).astype(BF16)
    qrow = q_pos0 % tk + lax.broadcasted_iota(jnp.int32, (tq, tk), 0)
    kcol = lax.broadcasted_iota(jnp.int32, (tq, tk), 1)
    bias_s[0] = jnp.zeros((tq, tk), F32)
    bias_s[1] = jnp.where(kcol < qrow, 0.0, MASKED_SCORE)
    for ref in (z_s, sp_s, c_s, w_s, carry_s, acc_s, o_ref):
        ref[...] = jnp.zeros(ref.shape, ref.dtype)

    def place(qi):
        return qi // nq_seq, pl.multiple_of((qi % nq_seq) * tq, tq)

    def diag_tile(qi):
        return (q_pos0 + (qi % nq_seq) * tq) // tk

    def key_tile(qi, p):
        tile = jnp.maximum(diag_tile(qi) - p, 0)
        new = pl.ds(pl.multiple_of(jnp.maximum(tile - n_past, 0) * tk, tk), tk)
        old = pl.ds(pl.multiple_of(jnp.minimum(tile, max(n_past - 1, 0)) * tk, tk), tk)
        return qi // nq_seq, tile >= n_past, new, old

    def head(x, h):
        return x[:, h * HEAD_DIM:(h + 1) * HEAD_DIM]

    def nt_dot(a, b):
        return lax.dot_general(a, b, (((1,), (1,)), ((), ())), preferred_element_type=F32)

    def scores(q2, qi, p):
        seq, is_new, new, old = key_tile(qi, p)
        k2 = k_ref[seq, new, :]
        z = [nt_dot(head(q2, h), head(k2, h)) for h in range(nh)]
        if n_past:
            z = [jnp.where(is_new, z[h], _dot(head(q2, h), kp_ref[0, seq, h, :, old].astype(BF16)))
                 for h in range(nh)]
        return z

    def weighted_values(w, qi, p):
        seq, is_new, new, old = key_tile(qi, p)
        v2 = v_ref[seq, new, :]
        pv = [_dot(w[h], head(v2, h)) for h in range(nh)]
        if n_past:
            pv = [jnp.where(is_new, pv[h], nt_dot(w[h], vp_ref[0, seq, h, :, old].astype(BF16)))
                  for h in range(nh)]
        return pv

    def sub_step(u, streams, next_q):
        qi, p, real, spent = streams[u]

        seq, r0 = place(qi)
        pv = weighted_values([w_s[u, h] for h in range(nh)], qi, p)
        acc = jnp.where(p == 0, 0.0, acc_s[u]) + jnp.concatenate(pv, axis=1)
        acc_s[u] = acc
        o_ref[seq, pl.ds(r0, tq), :] = jnp.where(
            real == 1, acc.astype(BF16), o_ref[seq, pl.ds(r0, tq), :])

        t = (u + 1) % N_STREAMS
        qi_t, p_t, real_t, _ = streams[t]
        keeps = jnp.logical_and(p_t != 0, real_t == 1)
        fresh = jnp.where(real_t == 1, 0.0, -MASKED_SCORE)
        totals = []
        for h in range(nh):
            c = c_s[t, h]
            base = jnp.where(keeps, carry_s[t, h], fresh)
            w_s[t, h] = jnp.exp(z_s[t, h] - c - base).astype(BF16)
            total = base + (c[:, 0:1] + sp_s[t, h, :, 0:1].astype(F32))
            carry_s[t, h] = total
            totals.append(total)
        low = jnp.min(functools.reduce(jnp.minimum, totals))
        streams[t] = (qi_t, p_t, real_t, (low > ZERO_WEIGHT_AT).astype(jnp.int32))

        t = (u + 2) % N_STREAMS
        for h in range(nh):
            c_s[t, h] = _dot(sp_s[t, h], tri_s[...])

        t = (u + 3) % N_STREAMS
        for h in range(nh):
            z = z_s[t, h]
            sp = _softplus(z)
            sp_s[t, h] = sp.astype(BF16)
            z_s[t, h] = z - sp

        done = jnp.logical_or(jnp.logical_or(real == 0, p >= diag_tile(qi)), spent == 1)
        take = jnp.logical_and(done, next_q < n_q)
        qi = jnp.where(take, next_q, qi)
        p = jnp.where(done, jnp.where(take, 0, 1), p + 1)
        real = jnp.where(done, take.astype(jnp.int32), 1)
        next_q = next_q + take.astype(jnp.int32)
        streams[u] = (qi, p, real, jnp.int32(0))

        seq, r0 = place(qi)
        q2 = q_ref[seq, pl.ds(r0, tq), :]
        bias = bias_s[jnp.where(p == 0, 1, 0)]
        z = scores(q2, qi, p)
        for h in range(nh):
            z_s[u, h] = bias + z[h]
        return streams, next_q

    def one_round(state):
        streams, next_q = list(state[0]), state[1]
        for u in range(N_STREAMS):
            streams, next_q = sub_step(u, streams, next_q)
        return tuple(streams), next_q

    def work_left(state):
        streams, next_q = state
        busy = next_q < n_q
        for _, _, real, _ in streams:
            busy = jnp.logical_or(busy, real == 1)
        return busy

    idle = (jnp.int32(0), jnp.int32(1), jnp.int32(0), jnp.int32(1))
    lax.while_loop(work_left, one_round, ((idle,) * N_STREAMS, jnp.int32(0)))


def _attn(q, k, v, tq, seqs_per_step, past=None):
    b, sq, d = q.shape
    sk = k.shape[1]
    tq = _row_tile(sq, tq)
    tk = min(sk, max(tq, MXU_WIDTH))
    nb = seqs_per_step
    q_pos0 = 0 if past is None else past[0].shape[-1]
    n_past = q_pos0 // tk
    assert sk % tk == 0 and q_pos0 % tk == 0 and sq <= sk and (tq == tk or sq == tq)
    assert b % nb == 0
    n_col = d // LANES
    nh = HEADS_PER_STEP
    q_spec = pl.BlockSpec((nb, sq, LANES), lambda i, c: (i, 0, c))
    kv_spec = pl.BlockSpec((nb, sk, LANES), lambda i, c: (i, 0, c))
    in_specs, args = [q_spec, kv_spec, kv_spec], [q, k, v]
    if past is not None:
        layer = past[2]
        in_specs += [pl.BlockSpec((1, nb, nh, HEAD_DIM, q_pos0),
                                  lambda i, c: (layer, i, c, 0, 0))] * 2
        args += list(past[:2])
    ring = lambda width, dt: pltpu.VMEM((N_STREAMS, nh, tq, width), dt)
    return pl.pallas_call(
        functools.partial(_attn_body, q_pos0, n_past),
        grid=(b // nb, n_col),
        in_specs=in_specs,
        out_specs=q_spec,
        out_shape=jax.ShapeDtypeStruct((b, sq, d), BF16),
        scratch_shapes=[pltpu.VMEM((2, tq, tk), F32), pltpu.VMEM((tk, tk), BF16),
                        ring(tk, F32), ring(tk, BF16), ring(tk, F32), ring(tk, BF16),
                        ring(1, F32), pltpu.VMEM((N_STREAMS, tq, nh * HEAD_DIM), F32)],
        compiler_params=_params(2),
        name="attn",
    )(*args)


def _merge_body(x_ref, y_ref, o_ref, sr_ref, sa_ref, wbr_ref, wba_ref, wo_ref, out_ref):
    m = (sr_ref[...].astype(F32) * _dot(y_ref[...], wbr_ref[...])
         + sa_ref[...].astype(F32) * _dot(o_ref[...], wba_ref[...]))
    out_ref[...] = x_ref[...] + _dot(m.astype(BF16), wo_ref[...])


def _merge(x, y, o, sr, sa, layer, wbr, wba, wo, tm):
    rows, d = x.shape
    tm = _row_tile(rows, tm)
    return pl.pallas_call(
        _merge_body,
        grid=(rows // tm,),
        in_specs=[_row_spec(tm, d)] * 5 + [_resident(layer, (d, d))] * 3,
        out_specs=_row_spec(tm, d),
        out_shape=jax.ShapeDtypeStruct((rows, d), F32),
        compiler_params=_params(1),
        name="merge",
    )(x, y, o, sr, sa, wbr, wba, wo)


def _block_diag(w):
    depth, n, c, _ = w.shape
    eye = jnp.eye(n, dtype=w.dtype)
    return (eye[:, None, :, None] * w[:, :, :, None, :]).reshape(depth, n * c, n * c)


def _layer(x, p_all, lw, past, h0, cbuf, k_all, v_all, layer, depth, final, tm, tt, tq, seqs):
    b, s, d = x.shape
    rows = b * s
    x = x.reshape(rows, d)
    x = _ffn(x, layer, lw["n_ffn1"], lw["f1_g"], lw["f1_u"], lw["f1_d"], tm)
    ux, gg, q, kb, vb, sr, sa, k_all, v_all = _inproj(x, lw["n_mix"], lw["w_in"], k_all, v_all,
                                                      layer, depth, b, s, tm)
    shp = (b, s, d)
    y, h_last, new_conv = _rnn(ux.reshape(shp), gg.reshape(shp), layer, lw["conv_w"], lw["conv_b"],
                               lw["wa"], lw["ba"], lw["wx"], lw["bx"], lw["lam"], h0, cbuf,
                               0 if past is None else past[0].shape[-1], tt)
    kb, vb = kb.reshape(shp), vb.reshape(shp)
    if past is not None:
        fill = ((0, 0), (0, (-s) % MXU_WIDTH), (0, 0))
        kb, vb = jnp.pad(kb, fill), jnp.pad(vb, fill)
    o = _attn(q.reshape(shp), kb, vb, tq, seqs, past)
    x = _merge(x, y.reshape(rows, d), o.reshape(rows, d), sr, sa, layer,
               lw["w_br"], lw["w_ba"], lw["w_out"], tm)
    x = _ffn(x, layer, lw["n_ffn2"], lw["f2_g"], lw["f2_u"], lw["f2_d"], tm,
             ple=(p_all.reshape(depth, rows, -1), lw["n_ple"], lw["ple_g"], lw["ple_p"], lw["final"],
                  final))
    return x.reshape(shp), k_all, v_all, h_last.reshape(b, d), new_conv


def kernel(x_prompt, x_sample, p_prompt, p_sample, cache_k, cache_v, state_h, state_conv, norm_ffn1, ffn1_w_gate, ffn1_w_up, ffn1_w_down, norm_mix, w_in, conv_w, conv_b, lru_w_a, lru_b_a, lru_w_x, lru_b_x, lru_lambda, w_branch_rnn, w_branch_attn, w_out, norm_ffn2, ffn2_w_gate, ffn2_w_up, ffn2_w_down, norm_ple, ple_w_gate, ple_w_proj, final_norm):
    depth = w_in.shape[0]
    b, s, d = x_prompt.shape
    bd, sd, _ = x_sample.shape
    keys_minor = lambda a: jnp.transpose(a, (0, 1, 3, 4, 2))
    row = lambda a: a.reshape(a.shape[0], 1, -1)
    cast = lambda a: a.astype(BF16)
    lw = dict(
        n_ffn1=row(norm_ffn1), f1_g=cast(ffn1_w_gate), f1_u=cast(ffn1_w_up), f1_d=cast(ffn1_w_down),
        n_mix=row(norm_mix), w_in=cast(w_in), conv_w=conv_w, conv_b=row(conv_b),
        wa=cast(_block_diag(lru_w_a)), ba=row(lru_b_a), wx=cast(_block_diag(lru_w_x)),
        bx=row(lru_b_x), lam=row(lru_lambda),
        w_br=cast(w_branch_rnn), w_ba=cast(w_branch_attn), w_out=cast(w_out),
        n_ffn2=row(norm_ffn2), f2_g=cast(ffn2_w_gate), f2_u=cast(ffn2_w_up), f2_d=cast(ffn2_w_down),
        n_ple=row(norm_ple), ple_g=cast(ple_w_gate), ple_p=cast(ple_w_proj),
        final=final_norm.reshape(1, 1, -1))
    xp, xs = x_prompt, x_sample
    kp = vp = ks = vs = None
    outs = [[] for _ in range(4)]
    for i in range(depth):
        final = i == depth - 1
        xp, kp, vp, hp, cp = _layer(
            xp, p_prompt, lw, None, jnp.zeros((b, 1, d), F32),
            jnp.zeros((b, CONV_W - 1, d), F32), kp, vp, i, depth, final,
            tm=512, tt=512, tq=256, seqs=1)
        xs, ks, vs, hs, cs = _layer(
            xs, p_sample, lw, (keys_minor(cache_k), keys_minor(cache_v), i), state_h[i].reshape(bd, 1, d),
            state_conv[i], ks, vs, i, depth, final, tm=bd * sd, tt=sd, tq=sd, seqs=bd)
        for lst, val in zip(outs, (hp, cp, hs, cs)):
            lst.append(val)
    hp, cp, hs, cs = (jnp.stack(lst) for lst in outs)
    seq_major = lambda a: jnp.transpose(a, (0, 1, 4, 2, 3))
    return (xp, xs, seq_major(kp), seq_major(vp), hp, cp, seq_major(ks), seq_major(vs), hs, cs)
```

```python
import functools

import jax
import jax.numpy as jnp
from jax import lax
from jax.experimental import pallas as pl
from jax.experimental.pallas import tpu as pltpu

N_HEADS = 16
HEAD_DIM = 64
N_LRU_BLOCKS = 16
LRU_C = 8.0
CONV_W = 4
RMS_EPS = 1e-6
N_IN_GROUPS = 7

LANES = 128
SUBLANES = 8
MXU_WIDTH = 256
HEADS_PER_STEP = LANES // HEAD_DIM
VMEM_LIMIT = 56 * 1024 * 1024

F32 = jnp.float32
BF16 = jnp.bfloat16


def _rms(x, g):
    return x * lax.rsqrt(jnp.mean(x * x, axis=-1, keepdims=True) + RMS_EPS) * g


def _dot(a, b):
    return jnp.dot(a, b, preferred_element_type=F32)


def _row_spec(tm, d):
    return pl.BlockSpec((tm, d), lambda i: (i, 0))


def _resident(layer, shape):
    return pl.BlockSpec((None,) + shape, lambda *_: (layer,) + (0,) * len(shape),
                        pipeline_mode=pl.Buffered(1))


def _sigmoid(x):
    return 0.5 * jnp.tanh(0.5 * x) + 0.5


def _params(n_axes):
    return pltpu.CompilerParams(dimension_semantics=("arbitrary",) * n_axes,
                                vmem_limit_bytes=VMEM_LIMIT)


def _row_tile(rows, want):
    tm = min(rows, want)
    assert rows % tm == 0
    return tm


def _ffn_body(with_ple, final, x_ref, g_ref, wg_ref, wu_ref, wd_ref, *rest):
    o_ref = rest[-1]
    x = x_ref[...]
    h = _rms(x, g_ref[...]).astype(BF16)
    gate = _dot(h, wg_ref[...])
    up = _dot(h, wu_ref[...])
    act = (gate * jax.nn.sigmoid(gate) * up).astype(BF16)
    x = x + 0.5 * _dot(act, wd_ref[...])
    if with_ple:
        p_ref, pg_ref, pwg_ref, pwp_ref, fg_ref = rest[:-1]
        gate = jax.nn.sigmoid(_dot(_rms(x, pg_ref[...]).astype(BF16), pwg_ref[...]))
        x = x + gate * _dot(p_ref[0].astype(BF16), pwp_ref[...])
        if final:
            x = _rms(x, fg_ref[...])
    o_ref[...] = x


def _ffn(x, layer, g, wg, wu, wd, tm, ple=None):
    rows, d = x.shape
    dff = wg.shape[-1]
    tm = _row_tile(rows, tm)
    in_specs = [_row_spec(tm, d), _resident(layer, (1, d)), _resident(layer, (d, dff)),
                _resident(layer, (d, dff)), _resident(layer, (dff, d))]
    args = [x, g, wg, wu, wd]
    final = False
    if ple is not None:
        p_all, pg, pwg, pwp, fg, final = ple
        dp = p_all.shape[-1]
        in_specs += [pl.BlockSpec((1, tm, dp), lambda i: (layer, i, 0)), _resident(layer, (1, d)),
                     _resident(layer, (d, d)), _resident(layer, (dp, d)), _resident(0, (1, d))]
        args += [p_all, pg, pwg, pwp, fg]
    return pl.pallas_call(
        functools.partial(_ffn_body, ple is not None, final),
        grid=(rows // tm,),
        in_specs=in_specs,
        out_specs=_row_spec(tm, d),
        out_shape=jax.ShapeDtypeStruct((rows, d), F32),
        compiler_params=_params(1),
        name="ffn",
    )(*args)


def _inproj_body(layer, x_ref, g_ref, w_ref, k_prev, v_prev, ux_ref, gg_ref, q_ref, kb_ref, vb_ref,
                 sr_ref, sa_ref, kt_ref, vt_ref):
    del k_prev, v_prev
    d = x_ref.shape[1]
    h = _rms(x_ref[...], g_ref[...]).astype(BF16)

    def col(i):
        return _dot(h, w_ref[:, i * d:(i + 1) * d])

    def put_transposed(ref, val):
        n_layers, n_seq, _, _, s = ref.shape
        mine = layer if n_layers > 1 else 0
        vt = val.T
        for b in range(n_seq):
            ref[mine, b] = vt[:, b * s:(b + 1) * s].reshape(N_HEADS, HEAD_DIM, s)
        for other in range(n_layers):
            if other != mine:
                ref[other] = jnp.zeros(ref.shape[1:], F32)

    sr_ref[...] = jax.nn.sigmoid(col(5)).astype(BF16)
    sa_ref[...] = jax.nn.sigmoid(col(6)).astype(BF16)
    gg_ref[...] = jax.nn.gelu(col(1)).astype(BF16)
    k = col(3)
    put_transposed(kt_ref, k)
    kb_ref[...] = k.astype(BF16)
    v = col(4)
    put_transposed(vt_ref, v)
    vb_ref[...] = v.astype(BF16)
    q_ref[...] = (col(2) * (HEAD_DIM ** -0.5)).astype(BF16)
    ux_ref[...] = col(0)


def _inproj(x, g, w_in, k_all, v_all, layer, depth, b, s, tm):
    rows, d = x.shape
    tm = _row_tile(rows, tm)
    assert tm % s == 0 or s % tm == 0
    n_seq, ts = max(tm // s, 1), min(tm, s)
    kv_shape = jax.ShapeDtypeStruct((depth, b, N_HEADS, HEAD_DIM, s), F32)
    dts = (F32, BF16, BF16, BF16, BF16, BF16, BF16)
    if k_all is None:
        k_all = v_all = jnp.zeros((1,), F32)
        aliases, n_layers, first = {}, depth, 0
    else:
        aliases, n_layers, first = {3: len(dts), 4: len(dts) + 1}, 1, layer
    kv_spec = pl.BlockSpec((n_layers, n_seq, N_HEADS, HEAD_DIM, ts),
                           lambda i: (first, i * tm // s, 0, 0, i % (s // ts)))
    return pl.pallas_call(
        functools.partial(_inproj_body, layer),
        grid=(rows // tm,),
        in_specs=[_row_spec(tm, d), _resident(layer, (1, d)),
                  _resident(layer, (d, N_IN_GROUPS * d)),
                  pl.BlockSpec(memory_space=pl.ANY), pl.BlockSpec(memory_space=pl.ANY)],
        out_specs=[_row_spec(tm, d)] * len(dts) + [kv_spec, kv_spec],
        out_shape=[jax.ShapeDtypeStruct((rows, d), t) for t in dts] + [kv_shape, kv_shape],
        input_output_aliases=aliases,
        compiler_params=_params(1),
        name="inproj",
    )(x, g, w_in, k_all, v_all)


def _rnn_body(pos0, ux_ref, gg_ref, cw_ref, cb_ref, wa_ref, ba_ref, wx_ref, bx_ref, lam_ref,
              h0_ref, cbuf_ref, y_ref, hl_ref, nc_ref, ubuf, a_s, b_s, h_s, hc):
    t_idx = pl.program_id(1)
    tt = ux_ref.shape[1]
    pad = ubuf.shape[0] - tt
    hist = CONV_W - 1

    @pl.when(t_idx == 0)
    def _():
        ubuf[pad - hist:pad, :] = cbuf_ref[0]
        hc[...] = h0_ref[0]

    @pl.when(t_idx != 0)
    def _():
        ubuf[0:pad, :] = ubuf[tt:tt + pad, :]

    u = ux_ref[0]
    ubuf[pad:pad + tt, :] = u
    first = lax.broadcasted_iota(jnp.int32, (SUBLANES, 1), 0) == 0

    def shift_down(x, row_before):
        rolled = pltpu.roll(x, 1, 0)
        return jnp.concatenate(
            [jnp.where(first, row_before, rolled[0:SUBLANES]), rolled[SUBLANES:]], axis=0)

    w = [cw_ref[j:j + 1, :] for j in range(CONV_W)]
    um1, um2, um3 = (ubuf[pad - j:pad - j + 1, :] for j in (1, 2, 3))
    acc = shift_down(u * w[0], um1 * w[0])
    acc = shift_down(u * w[1] + acc, um1 * w[1] + um2 * w[0])
    acc = shift_down(u * w[2] + acc, um1 * w[2] + (um2 * w[1] + um3 * w[0]))
    xc = cb_ref[...] + (u * w[3] + acc)

    xb = xc.astype(BF16)
    r = _sigmoid(_dot(xb, wa_ref[...]) + ba_ref[...])
    gate_i = _sigmoid(_dot(xb, wx_ref[...]) + bx_ref[...])
    lam = lam_ref[...]
    softplus_neg_lam = jnp.maximum(-lam, 0.0) + jnp.log1p(jnp.exp(-jnp.abs(lam)))
    log_a = (-LRU_C * softplus_neg_lam) * r
    a_s[...] = jnp.exp(log_a)
    th = jnp.tanh(log_a)
    sq = -2.0 * th / (1.0 - th)
    gated = gate_i * xc
    b_s[...] = jnp.where(sq > 0.0, sq * lax.rsqrt(sq), 0.0) * gated
    if pos0 == 0:
        @pl.when(t_idx == 0)
        def _():
            a_s[0:1, :] = jnp.zeros((1, a_s.shape[1]), F32)
            b_s[0:1, :] = gated[0:1, :]

    def step(t, h):
        h = a_s[pl.ds(t, 1), :] * h + b_s[pl.ds(t, 1), :]
        h_s[pl.ds(t, 1), :] = h
        return h

    h_end = lax.fori_loop(0, tt, step, hc[...], unroll=8)
    hc[...] = h_end
    hl_ref[0] = h_end
    nc_ref[0] = ubuf[pad + tt - hist:pad + tt, :]
    y_ref[0] = (h_s[...] * gg_ref[0].astype(F32)).astype(BF16)


def _rnn(ux, gg, layer, cw, cb, wa, ba, wx, bx, lam, h0, cbuf, pos0, tt):
    b, s, d = ux.shape
    tt = _row_tile(s, tt)
    pad = SUBLANES
    tile = pl.BlockSpec((1, tt, d), lambda i, t: (i, t, 0))
    per_b = lambda n: pl.BlockSpec((1, n, d), lambda i, t: (i, 0, 0))
    return pl.pallas_call(
        functools.partial(_rnn_body, pos0),
        grid=(b, s // tt),
        in_specs=[tile, tile, _resident(layer, (CONV_W, d)), _resident(layer, (1, d)),
                  _resident(layer, (d, d)), _resident(layer, (1, d)), _resident(layer, (d, d)),
                  _resident(layer, (1, d)), _resident(layer, (1, d)),
                  per_b(1), per_b(CONV_W - 1)],
        out_specs=[tile, per_b(1), per_b(CONV_W - 1)],
        out_shape=[jax.ShapeDtypeStruct((b, s, d), BF16),
                   jax.ShapeDtypeStruct((b, 1, d), F32),
                   jax.ShapeDtypeStruct((b, CONV_W - 1, d), F32)],
        scratch_shapes=[pltpu.VMEM((tt + pad, d), F32), pltpu.VMEM((tt, d), F32),
                        pltpu.VMEM((tt, d), F32), pltpu.VMEM((tt, d), F32),
                        pltpu.VMEM((1, d), F32)],
        compiler_params=_params(2),
        name="rnn",
    )(ux, gg, cw, cb, wa, ba, wx, bx, lam, h0, cbuf)


MASKED_SCORE = -1e30
N_STREAMS = 4
ZERO_WEIGHT_AT = 104.0


def _softplus(z):
    return jnp.maximum(z, 0.0) + jnp.log(1.0 + jnp.exp(-jnp.abs(z)))


def _attn_body(q_pos0, n_past, q_ref, k_ref, v_ref, *rest):
    if n_past:
        kp_ref, vp_ref = rest[:2]
        rest = rest[2:]
    o_ref, bias_s, tri_s, z_s, sp_s, c_s, w_s, carry_s, acc_s = rest
    n_seq, sq, _ = q_ref.shape
    tq, tk = bias_s.shape[1:]
    nq_seq = sq // tq
    n_q = n_seq * nq_seq
    nh = HEADS_PER_STEP

    row = lax.broadcasted_iota(jnp.int32, (tk, tk), 0)
    col = lax.broadcasted_iota(jnp.int32, (tk, tk), 1)
    tri_s[...] = jnp.where(row > col, 1.0, 0.0).astype(BF16)
    qrow = q_pos0 % tk + lax.broadcasted_iota(jnp.int32, (tq, tk), 0)
    kcol = lax.broadcasted_iota(jnp.int32, (tq, tk), 1)
    bias_s[0] = jnp.zeros((tq, tk), F32)
    bias_s[1] = jnp.where(kcol < qrow, 0.0, MASKED_SCORE)
    for ref in (z_s, sp_s, c_s, w_s, carry_s, acc_s, o_ref):
        ref[...] = jnp.zeros(ref.shape, ref.dtype)

    def place(qi):
        return qi // nq_seq, pl.multiple_of((qi % nq_seq) * tq, tq)

    def diag_tile(qi):
        return (q_pos0 + (qi % nq_seq) * tq) // tk

    def key_tile(qi, p):
        tile = jnp.maximum(diag_tile(qi) - p, 0)
        new = pl.ds(pl.multiple_of(jnp.maximum(tile - n_past, 0) * tk, tk), tk)
        old = pl.ds(pl.multiple_of(jnp.minimum(tile, max(n_past - 1, 0)) * tk, tk), tk)
        return qi // nq_seq, tile >= n_past, new, old

    def head(x, h):
        return x[:, h * HEAD_DIM:(h + 1) * HEAD_DIM]

    def nt_dot(a, b):
        return lax.dot_general(a, b, (((1,), (1,)), ((), ())), preferred_element_type=F32)

    def scores(q2, qi, p):
        seq, is_new, new, old = key_tile(qi, p)
        k2 = k_ref[seq, new, :]
        z = [nt_dot(head(q2, h), head(k2, h)) for h in range(nh)]
        if n_past:
            z = [jnp.where(is_new, z[h], _dot(head(q2, h), kp_ref[0, seq, h, :, old].astype(BF16)))
                 for h in range(nh)]
        return z

    def weighted_values(w, qi, p):
        seq, is_new, new, old = key_tile(qi, p)
        v2 = v_ref[seq, new, :]
        pv = [_dot(w[h], head(v2, h)) for h in range(nh)]
        if n_past:
            pv = [jnp.where(is_new, pv[h], nt_dot(w[h], vp_ref[0, seq, h, :, old].astype(BF16)))
                  for h in range(nh)]
        return pv

    def sub_step(u, streams, next_q):
        qi, p, real, spent = streams[u]

        seq, r0 = place(qi)
        pv = weighted_values([w_s[u, h] for h in range(nh)], qi, p)
        acc = jnp.where(p == 0, 0.0, acc_s[u]) + jnp.concatenate(pv, axis=1)
        acc_s[u] = acc
        o_ref[seq, pl.ds(r0, tq), :] = jnp.where(
            real == 1, acc.astype(BF16), o_ref[seq, pl.ds(r0, tq), :])

        t = (u + 1) % N_STREAMS
        qi_t, p_t, real_t, _ = streams[t]
        keeps = jnp.logical_and(p_t != 0, real_t == 1)
        fresh = jnp.where(real_t == 1, 0.0, -MASKED_SCORE)
        totals = []
        for h in range(nh):
            c = c_s[t, h]
            base = jnp.where(keeps, carry_s[t, h], fresh)
            w_s[t, h] = jnp.exp(z_s[t, h] - c - base).astype(BF16)
            total = base + (c[:, 0:1] + sp_s[t, h, :, 0:1].astype(F32))
            carry_s[t, h] = total
            totals.append(total)
        low = jnp.min(functools.reduce(jnp.minimum, totals))
        streams[t] = (qi_t, p_t, real_t, (low > ZERO_WEIGHT_AT).astype(jnp.int32))

        t = (u + 2) % N_STREAMS
        for h in range(nh):
            c_s[t, h] = _dot(sp_s[t, h], tri_s[...])

        t = (u + 3) % N_STREAMS
        for h in range(nh):
            z = z_s[t, h]
            sp = _softplus(z)
            sp_s[t, h] = sp.astype(BF16)
            z_s[t, h] = z - sp

        done = jnp.logical_or(jnp.logical_or(real == 0, p >= diag_tile(qi)), spent == 1)
        take = jnp.logical_and(done, next_q < n_q)
        qi = jnp.where(take, next_q, qi)
        p = jnp.where(done, jnp.where(take, 0, 1), p + 1)
        real = jnp.where(done, take.astype(jnp.int32), 1)
        next_q = next_q + take.astype(jnp.int32)
        streams[u] = (qi, p, real, jnp.int32(0))

        seq, r0 = place(qi)
        q2 = q_ref[seq, pl.ds(r0, tq), :]
        bias = bias_s[jnp.where(p == 0, 1, 0)]
        z = scores(q2, qi, p)
        for h in range(nh):
            z_s[u, h] = bias + z[h]
        return streams, next_q

    def one_round(state):
        streams, next_q = list(state[0]), state[1]
        for u in range(N_STREAMS):
            streams, next_q = sub_step(u, streams, next_q)
        return tuple(streams), next_q

    def work_left(state):
        streams, next_q = state
        busy = next_q < n_q
        for _, _, real, _ in streams:
            busy = jnp.logical_or(busy, real == 1)
        return busy

    idle = (jnp.int32(0), jnp.int32(1), jnp.int32(0), jnp.int32(1))
    lax.while_loop(work_left, one_round, ((idle,) * N_STREAMS, jnp.int32(0)))


def _attn(q, k, v, tq, seqs_per_step, past=None):
    b, sq, d = q.shape
    sk = k.shape[1]
    tq = _row_tile(sq, tq)
    tk = min(sk, max(tq, MXU_WIDTH))
    nb = seqs_per_step
    q_pos0 = 0 if past is None else past[0].shape[-1]
    n_past = q_pos0 // tk
    assert sk % tk == 0 and q_pos0 % tk == 0 and sq <= sk and (tq == tk or sq == tq)
    assert b % nb == 0
    n_col = d // LANES
    nh = HEADS_PER_STEP
    q_spec = pl.BlockSpec((nb, sq, LANES), lambda i, c: (i, 0, c))
    kv_spec = pl.BlockSpec((nb, sk, LANES), lambda i, c: (i, 0, c))
    in_specs, args = [q_spec, kv_spec, kv_spec], [q, k, v]
    if past is not None:
        layer = past[2]
        in_specs += [pl.BlockSpec((1, nb, nh, HEAD_DIM, q_pos0),
                                  lambda i, c: (layer, i, c, 0, 0))] * 2
        args += list(past[:2])
    ring = lambda width, dt: pltpu.VMEM((N_STREAMS, nh, tq, width), dt)
    return pl.pallas_call(
        functools.partial(_attn_body, q_pos0, n_past),
        grid=(b // nb, n_col),
        in_specs=in_specs,
        out_specs=q_spec,
        out_shape=jax.ShapeDtypeStruct((b, sq, d), BF16),
        scratch_shapes=[pltpu.VMEM((2, tq, tk), F32), pltpu.VMEM((tk, tk), BF16),
                        ring(tk, F32), ring(tk, BF16), ring(tk, F32), ring(tk, BF16),
                        ring(1, F32), pltpu.VMEM((N_STREAMS, tq, nh * HEAD_DIM), F32)],
        compiler_params=_params(2),
        name="attn",
    )(*args)


def _merge_body(x_ref, y_ref, o_ref, sr_ref, sa_ref, wbr_ref, wba_ref, wo_ref, out_ref):
    m = (sr_ref[...].astype(F32) * _dot(y_ref[...], wbr_ref[...])
         + sa_ref[...].astype(F32) * _dot(o_ref[...], wba_ref[...]))
    out_ref[...] = x_ref[...] + _dot(m.astype(BF16), wo_ref[...])


def _merge(x, y, o, sr, sa, layer, wbr, wba, wo, tm):
    rows, d = x.shape
    tm = _row_tile(rows, tm)
    return pl.pallas_call(
        _merge_body,
        grid=(rows // tm,),
        in_specs=[_row_spec(tm, d)] * 5 + [_resident(layer, (d, d))] * 3,
        out_specs=_row_spec(tm, d),
        out_shape=jax.ShapeDtypeStruct((rows, d), F32),
        compiler_params=_params(1),
        name="merge",
    )(x, y, o, sr, sa, wbr, wba, wo)


def _block_diag(w):
    depth, n, c, _ = w.shape
    eye = jnp.eye(n, dtype=w.dtype)
    return (eye[:, None, :, None] * w[:, :, :, None, :]).reshape(depth, n * c, n * c)


def _layer(x, p_all, lw, past, h0, cbuf, k_all, v_all, layer, depth, final, tm, tt, tq, seqs):
    b, s, d = x.shape
    rows = b * s
    x = x.reshape(rows, d)
    x = _ffn(x, layer, lw["n_ffn1"], lw["f1_g"], lw["f1_u"], lw["f1_d"], tm)
    ux, gg, q, kb, vb, sr, sa, k_all, v_all = _inproj(x, lw["n_mix"], lw["w_in"], k_all, v_all,
                                                      layer, depth, b, s, tm)
    shp = (b, s, d)
    y, h_last, new_conv = _rnn(ux.reshape(shp), gg.reshape(shp), layer, lw["conv_w"], lw["conv_b"],
                               lw["wa"], lw["ba"], lw["wx"], lw["bx"], lw["lam"], h0, cbuf,
                               0 if past is None else past[0].shape[-1], tt)
    kb, vb = kb.reshape(shp), vb.reshape(shp)
    if past is not None:
        fill = ((0, 0), (0, (-s) % MXU_WIDTH), (0, 0))
        kb, vb = jnp.pad(kb, fill), jnp.pad(vb, fill)
    o = _attn(q.reshape(shp), kb, vb, tq, seqs, past)
    x = _merge(x, y.reshape(rows, d), o.reshape(rows, d), sr, sa, layer,
               lw["w_br"], lw["w_ba"], lw["w_out"], tm)
    x = _ffn(x, layer, lw["n_ffn2"], lw["f2_g"], lw["f2_u"], lw["f2_d"], tm,
             ple=(p_all.reshape(depth, rows, -1), lw["n_ple"], lw["ple_g"], lw["ple_p"], lw["final"],
                  final))
    return x.reshape(shp), k_all, v_all, h_last.reshape(b, d), new_conv


def kernel(x_prompt, x_sample, p_prompt, p_sample, cache_k, cache_v, state_h, state_conv, norm_ffn1, ffn1_w_gate, ffn1_w_up, ffn1_w_down, norm_mix, w_in, conv_w, conv_b, lru_w_a, lru_b_a, lru_w_x, lru_b_x, lru_lambda, w_branch_rnn, w_branch_attn, w_out, norm_ffn2, ffn2_w_gate, ffn2_w_up, ffn2_w_down, norm_ple, ple_w_gate, ple_w_proj, final_norm):
    depth = w_in.shape[0]
    b, s, d = x_prompt.shape
    bd, sd, _ = x_sample.shape
    keys_minor = lambda a: jnp.transpose(a, (0, 1, 3, 4, 2))
    row = lambda a: a.reshape(a.shape[0], 1, -1)
    cast = lambda a: a.astype(BF16)
    lw = dict(
        n_ffn1=row(norm_ffn1), f1_g=cast(ffn1_w_gate), f1_u=cast(ffn1_w_up), f1_d=cast(ffn1_w_down),
        n_mix=row(norm_mix), w_in=cast(w_in), conv_w=conv_w, conv_b=row(conv_b),
        wa=cast(_block_diag(lru_w_a)), ba=row(lru_b_a), wx=cast(_block_diag(lru_w_x)),
        bx=row(lru_b_x), lam=row(lru_lambda),
        w_br=cast(w_branch_rnn), w_ba=cast(w_branch_attn), w_out=cast(w_out),
        n_ffn2=row(norm_ffn2), f2_g=cast(ffn2_w_gate), f2_u=cast(ffn2_w_up), f2_d=cast(ffn2_w_down),
        n_ple=row(norm_ple), ple_g=cast(ple_w_gate), ple_p=cast(ple_w_proj),
        final=final_norm.reshape(1, 1, -1))
    xp, xs = x_prompt, x_sample
    kp = vp = ks = vs = None
    outs = [[] for _ in range(4)]
    for i in range(depth):
        final = i == depth - 1
        xp, kp, vp, hp, cp = _layer(
            xp, p_prompt, lw, None, jnp.zeros((b, 1, d), F32),
            jnp.zeros((b, CONV_W - 1, d), F32), kp, vp, i, depth, final,
            tm=512, tt=512, tq=256, seqs=2)
        xs, ks, vs, hs, cs = _layer(
            xs, p_sample, lw, (keys_minor(cache_k), keys_minor(cache_v), i), state_h[i].reshape(bd, 1, d),
            state_conv[i], ks, vs, i, depth, final, tm=bd * sd, tt=sd, tq=sd, seqs=bd)
        for lst, val in zip(outs, (hp, cp, hs, cs)):
            lst.append(val)
    hp, cp, hs, cs = (jnp.stack(lst) for lst in outs)
    seq_major = lambda a: jnp.transpose(a, (0, 1, 4, 2, 3))
    return (xp, xs, seq_major(kp), seq_major(vp), hp, cp, seq_major(ks), seq_major(vs), hs, cs)
```

```python
import functools

import jax
import jax.numpy as jnp
from jax import lax
from jax.experimental import pallas as pl
from jax.experimental.pallas import tpu as pltpu

N_HEADS = 16
HEAD_DIM = 64
N_LRU_BLOCKS = 16
LRU_C = 8.0
CONV_W = 4
RMS_EPS = 1e-6
N_IN_GROUPS = 7

LANES = 128
SUBLANES = 8
MXU_WIDTH = 256
HEADS_PER_STEP = LANES // HEAD_DIM
VMEM_LIMIT = 56 * 1024 * 1024

F32 = jnp.float32
BF16 = jnp.bfloat16


def _rms(x, g):
    return x * lax.rsqrt(jnp.mean(x * x, axis=-1, keepdims=True) + RMS_EPS) * g


def _dot(a, b):
    return jnp.dot(a, b, preferred_element_type=F32)


def _row_spec(tm, d):
    return pl.BlockSpec((tm, d), lambda i: (i, 0))


def _resident(layer, shape):
    return pl.BlockSpec((None,) + shape, lambda *_: (layer,) + (0,) * len(shape),
                        pipeline_mode=pl.Buffered(1))


def _sigmoid(x):
    return 0.5 * jnp.tanh(0.5 * x) + 0.5


def _params(n_axes):
    return pltpu.CompilerParams(dimension_semantics=("arbitrary",) * n_axes,
                                vmem_limit_bytes=VMEM_LIMIT)


def _row_tile(rows, want):
    tm = min(rows, want)
    assert rows % tm == 0
    return tm


def _ffn_body(with_ple, final, x_ref, g_ref, wg_ref, wu_ref, wd_ref, *rest):
    o_ref = rest[-1]
    x = x_ref[...]
    h = _rms(x, g_ref[...]).astype(BF16)
    gate = _dot(h, wg_ref[...])
    up = _dot(h, wu_ref[...])
    act = (gate * jax.nn.sigmoid(gate) * up).astype(BF16)
    x = x + 0.5 * _dot(act, wd_ref[...])
    if with_ple:
        p_ref, pg_ref, pwg_ref, pwp_ref, fg_ref = rest[:-1]
        gate = jax.nn.sigmoid(_dot(_rms(x, pg_ref[...]).astype(BF16), pwg_ref[...]))
        x = x + gate * _dot(p_ref[0].astype(BF16), pwp_ref[...])
        if final:
            x = _rms(x, fg_ref[...])
    o_ref[...] = x


def _ffn(x, layer, g, wg, wu, wd, tm, ple=None):
    rows, d = x.shape
    dff = wg.shape[-1]
    tm = _row_tile(rows, tm)
    in_specs = [_row_spec(tm, d), _resident(layer, (1, d)), _resident(layer, (d, dff)),
                _resident(layer, (d, dff)), _resident(layer, (dff, d))]
    args = [x, g, wg, wu, wd]
    final = False
    if ple is not None:
        p_all, pg, pwg, pwp, fg, final = ple
        dp = p_all.shape[-1]
        in_specs += [pl.BlockSpec((1, tm, dp), lambda i: (layer, i, 0)), _resident(layer, (1, d)),
                     _resident(layer, (d, d)), _resident(layer, (dp, d)), _resident(0, (1, d))]
        args += [p_all, pg, pwg, pwp, fg]
    return pl.pallas_call(
        functools.partial(_ffn_body, ple is not None, final),
        grid=(rows // tm,),
        in_specs=in_specs,
        out_specs=_row_spec(tm, d),
        out_shape=jax.ShapeDtypeStruct((rows, d), F32),
        compiler_params=_params(1),
        name="ffn",
    )(*args)


def _inproj_body(layer, x_ref, g_ref, w_ref, k_prev, v_prev, ux_ref, gg_ref, q_ref, kb_ref, vb_ref,
                 sr_ref, sa_ref, kt_ref, vt_ref):
    del k_prev, v_prev
    d = x_ref.shape[1]
    h = _rms(x_ref[...], g_ref[...]).astype(BF16)

    def col(i):
        return _dot(h, w_ref[:, i * d:(i + 1) * d])

    def put_transposed(ref, val):
        n_layers, n_seq, _, _, s = ref.shape
        mine = layer if n_layers > 1 else 0
        vt = val.T
        for b in range(n_seq):
            ref[mine, b] = vt[:, b * s:(b + 1) * s].reshape(N_HEADS, HEAD_DIM, s)
        for other in range(n_layers):
            if other != mine:
                ref[other] = jnp.zeros(ref.shape[1:], F32)

    sr_ref[...] = jax.nn.sigmoid(col(5)).astype(BF16)
    sa_ref[...] = jax.nn.sigmoid(col(6)).astype(BF16)
    gg_ref[...] = jax.nn.gelu(col(1)).astype(BF16)
    k = col(3)
    put_transposed(kt_ref, k)
    kb_ref[...] = k.astype(BF16)
    v = col(4)
    put_transposed(vt_ref, v)
    vb_ref[...] = v.astype(BF16)
    q_ref[...] = (col(2) * (HEAD_DIM ** -0.5)).astype(BF16)
    ux_ref[...] = col(0)


def _inproj(x, g, w_in, k_all, v_all, layer, depth, b, s, tm):
    rows, d = x.shape
    tm = _row_tile(rows, tm)
    assert tm % s == 0 or s % tm == 0
    n_seq, ts = max(tm // s, 1), min(tm, s)
    kv_shape = jax.ShapeDtypeStruct((depth, b, N_HEADS, HEAD_DIM, s), F32)
    dts = (F32, BF16, BF16, BF16, BF16, BF16, BF16)
    if k_all is None:
        k_all = v_all = jnp.zeros((1,), F32)
        aliases, n_layers, first = {}, depth, 0
    else:
        aliases, n_layers, first = {3: len(dts), 4: len(dts) + 1}, 1, layer
    kv_spec = pl.BlockSpec((n_layers, n_seq, N_HEADS, HEAD_DIM, ts),
                           lambda i: (first, i * tm // s, 0, 0, i % (s // ts)))
    return pl.pallas_call(
        functools.partial(_inproj_body, layer),
        grid=(rows // tm,),
        in_specs=[_row_spec(tm, d), _resident(layer, (1, d)),
                  _resident(layer, (d, N_IN_GROUPS * d)),
                  pl.BlockSpec(memory_space=pl.ANY), pl.BlockSpec(memory_space=pl.ANY)],
        out_specs=[_row_spec(tm, d)] * len(dts) + [kv_spec, kv_spec],
        out_shape=[jax.ShapeDtypeStruct((rows, d), t) for t in dts] + [kv_shape, kv_shape],
        input_output_aliases=aliases,
        compiler_params=_params(1),
        name="inproj",
    )(x, g, w_in, k_all, v_all)


def _rnn_body(pos0, ux_ref, gg_ref, cw_ref, cb_ref, wa_ref, ba_ref, wx_ref, bx_ref, lam_ref,
              h0_ref, cbuf_ref, y_ref, hl_ref, nc_ref, ubuf, a_s, b_s, h_s, hc):
    t_idx = pl.program_id(1)
    tt = ux_ref.shape[1]
    pad = ubuf.shape[0] - tt
    hist = CONV_W - 1

    @pl.when(t_idx == 0)
    def _():
        ubuf[pad - hist:pad, :] = cbuf_ref[0]
        hc[...] = h0_ref[0]

    @pl.when(t_idx != 0)
    def _():
        ubuf[0:pad, :] = ubuf[tt:tt + pad, :]

    u = ux_ref[0]
    ubuf[pad:pad + tt, :] = u
    first = lax.broadcasted_iota(jnp.int32, (SUBLANES, 1), 0) == 0

    def shift_down(x, row_before):
        rolled = pltpu.roll(x, 1, 0)
        return jnp.concatenate(
            [jnp.where(first, row_before, rolled[0:SUBLANES]), rolled[SUBLANES:]], axis=0)

    w = [cw_ref[j:j + 1, :] for j in range(CONV_W)]
    um1, um2, um3 = (ubuf[pad - j:pad - j + 1, :] for j in (1, 2, 3))
    acc = shift_down(u * w[0], um1 * w[0])
    acc = shift_down(u * w[1] + acc, um1 * w[1] + um2 * w[0])
    acc = shift_down(u * w[2] + acc, um1 * w[2] + (um2 * w[1] + um3 * w[0]))
    xc = cb_ref[...] + (u * w[3] + acc)

    xb = xc.astype(BF16)
    r = _sigmoid(_dot(xb, wa_ref[...]) + ba_ref[...])
    gate_i = _sigmoid(_dot(xb, wx_ref[...]) + bx_ref[...])
    lam = lam_ref[...]
    softplus_neg_lam = jnp.maximum(-lam, 0.0) + jnp.log1p(jnp.exp(-jnp.abs(lam)))
    log_a = (-LRU_C * softplus_neg_lam) * r
    a_s[...] = jnp.exp(log_a)
    th = jnp.tanh(log_a)
    sq = -2.0 * th / (1.0 - th)
    gated = gate_i * xc
    b_s[...] = jnp.where(sq > 0.0, sq * lax.rsqrt(sq), 0.0) * gated
    if pos0 == 0:
        @pl.when(t_idx == 0)
        def _():
            a_s[0:1, :] = jnp.zeros((1, a_s.shape[1]), F32)
            b_s[0:1, :] = gated[0:1, :]

    def step(t, h):
        h = a_s[pl.ds(t, 1), :] * h + b_s[pl.ds(t, 1), :]
        h_s[pl.ds(t, 1), :] = h
        return h

    h_end = lax.fori_loop(0, tt, step, hc[...], unroll=8)
    hc[...] = h_end
    hl_ref[0] = h_end
    nc_ref[0] = ubuf[pad + tt - hist:pad + tt, :]
    y_ref[0] = (h_s[...] * gg_ref[0].astype(F32)).astype(BF16)


def _rnn(ux, gg, layer, cw, cb, wa, ba, wx, bx, lam, h0, cbuf, pos0, tt):
    b, s, d = ux.shape
    tt = _row_tile(s, tt)
    pad = SUBLANES
    tile = pl.BlockSpec((1, tt, d), lambda i, t: (i, t, 0))
    per_b = lambda n: pl.BlockSpec((1, n, d), lambda i, t: (i, 0, 0))
    return pl.pallas_call(
        functools.partial(_rnn_body, pos0),
        grid=(b, s // tt),
        in_specs=[tile, tile, _resident(layer, (CONV_W, d)), _resident(layer, (1, d)),
                  _resident(layer, (d, d)), _resident(layer, (1, d)), _resident(layer, (d, d)),
                  _resident(layer, (1, d)), _resident(layer, (1, d)),
                  per_b(1), per_b(CONV_W - 1)],
        out_specs=[tile, per_b(1), per_b(CONV_W - 1)],
        out_shape=[jax.ShapeDtypeStruct((b, s, d), BF16),
                   jax.ShapeDtypeStruct((b, 1, d), F32),
                   jax.ShapeDtypeStruct((b, CONV_W - 1, d), F32)],
        scratch_shapes=[pltpu.VMEM((tt + pad, d), F32), pltpu.VMEM((tt, d), F32),
                        pltpu.VMEM((tt, d), F32), pltpu.VMEM((tt, d), F32),
                        pltpu.VMEM((1, d), F32)],
        compiler_params=_params(2),
        name="rnn",
    )(ux, gg, cw, cb, wa, ba, wx, bx, lam, h0, cbuf)


MASKED_SCORE = -1e30
N_STREAMS = 4
ZERO_WEIGHT_AT = 104.0


def _softplus(z):
    return jnp.maximum(z, 0.0) + jnp.log(1.0 + jnp.exp(-jnp.abs(z)))


def _attn_body(q_pos0, n_past, q_ref, k_ref, v_ref, *rest):
    if n_past:
        kp_ref, vp_ref = rest[:2]
        rest = rest[2:]
    o_ref, cut_ref, bias_s, tri_s, z_s, sp_s, c_s, w_s, carry_s, acc_s = rest
    n_seq, sq, _ = q_ref.shape
    tq, tk = bias_s.shape[1:]
    nq_seq = sq // tq
    n_q = n_seq * nq_seq
    nh = HEADS_PER_STEP

    row = lax.broadcasted_iota(jnp.int32, (tk, tk), 0)
    col = lax.broadcasted_iota(jnp.int32, (tk, tk), 1)
    tri_s[...] = jnp.where(row > col, 1.0, 0.0).astype(BF16)
    qrow = q_pos0 % tk + lax.broadcasted_iota(jnp.int32, (tq, tk), 0)
    kcol = lax.broadcasted_iota(jnp.int32, (tq, tk), 1)
    bias_s[0] = jnp.zeros((tq, tk), F32)
    bias_s[1] = jnp.where(kcol < qrow, 0.0, MASKED_SCORE)
    for ref in (z_s, sp_s, c_s, w_s, carry_s, acc_s, o_ref):
        ref[...] = jnp.zeros(ref.shape, ref.dtype)

    def place(qi):
        return qi // nq_seq, pl.multiple_of((qi % nq_seq) * tq, tq)

    def diag_tile(qi):
        return (q_pos0 + (qi % nq_seq) * tq) // tk

    def key_tile(qi, p):
        tile = jnp.maximum(diag_tile(qi) - p, 0)
        new = pl.ds(pl.multiple_of(jnp.maximum(tile - n_past, 0) * tk, tk), tk)
        old = pl.ds(pl.multiple_of(jnp.minimum(tile, max(n_past - 1, 0)) * tk, tk), tk)
        return qi // nq_seq, tile >= n_past, new, old

    def head(x, h):
        return x[:, h * HEAD_DIM:(h + 1) * HEAD_DIM]

    def nt_dot(a, b):
        return lax.dot_general(a, b, (((1,), (1,)), ((), ())), preferred_element_type=F32)

    def scores(q2, qi, p):
        seq, is_new, new, old = key_tile(qi, p)
        k2 = k_ref[seq, new, :]
        z = [nt_dot(head(q2, h), head(k2, h)) for h in range(nh)]
        if n_past:
            z = [jnp.where(is_new, z[h], _dot(head(q2, h), kp_ref[0, seq, h, :, old].astype(BF16)))
                 for h in range(nh)]
        return z

    def weighted_values(w, qi, p):
        seq, is_new, new, old = key_tile(qi, p)
        v2 = v_ref[seq, new, :]
        pv = [_dot(w[h], head(v2, h)) for h in range(nh)]
        if n_past:
            pv = [jnp.where(is_new, pv[h], nt_dot(w[h], vp_ref[0, seq, h, :, old].astype(BF16)))
                  for h in range(nh)]
        return pv

    def sub_step(u, streams, next_q, cut):
        qi, p, real, spent = streams[u]

        seq, r0 = place(qi)
        pv = weighted_values([w_s[u, h] for h in range(nh)], qi, p)
        acc = jnp.where(p == 0, 0.0, acc_s[u]) + jnp.concatenate(pv, axis=1)
        acc_s[u] = acc
        o_ref[seq, pl.ds(r0, tq), :] = jnp.where(
            real == 1, acc.astype(BF16), o_ref[seq, pl.ds(r0, tq), :])

        t = (u + 1) % N_STREAMS
        qi_t, p_t, real_t, _ = streams[t]
        keeps = jnp.logical_and(p_t != 0, real_t == 1)
        fresh = jnp.where(real_t == 1, 0.0, -MASKED_SCORE)
        totals = []
        for h in range(nh):
            c = c_s[t, h]
            base = jnp.where(keeps, carry_s[t, h], fresh)
            w_s[t, h] = jnp.exp(z_s[t, h] - c - base).astype(BF16)
            total = base + (c[:, 0:1] + sp_s[t, h, :, 0:1].astype(F32))
            carry_s[t, h] = total
            totals.append(total)
        low = jnp.min(functools.reduce(jnp.minimum, totals))
        streams[t] = (qi_t, p_t, real_t, (low > ZERO_WEIGHT_AT).astype(jnp.int32))

        t = (u + 2) % N_STREAMS
        for h in range(nh):
            c_s[t, h] = _dot(sp_s[t, h], tri_s[...])

        t = (u + 3) % N_STREAMS
        for h in range(nh):
            z = z_s[t, h]
            sp = _softplus(z)
            sp_s[t, h] = sp.astype(BF16)
            z_s[t, h] = z - sp

        done = jnp.logical_or(jnp.logical_or(real == 0, p >= diag_tile(qi)), spent == 1)
        cut = jnp.logical_or(cut, jnp.logical_and(real == 1, jnp.logical_and(done, spent == 0)))
        take = jnp.logical_and(done, next_q < n_q)
        qi = jnp.where(take, next_q, qi)
        p = jnp.where(done, jnp.where(take, 0, 1), p + 1)
        real = jnp.where(done, take.astype(jnp.int32), 1)
        next_q = next_q + take.astype(jnp.int32)
        streams[u] = (qi, p, real, jnp.int32(0))

        seq, r0 = place(qi)
        q2 = q_ref[seq, pl.ds(r0, tq), :]
        bias = bias_s[jnp.where(p == 0, 1, 0)]
        z = scores(q2, qi, p)
        for h in range(nh):
            z_s[u, h] = bias + z[h]
        return streams, next_q, cut

    def one_round(state):
        streams, next_q, cut = list(state[0]), state[1], state[2]
        for u in range(N_STREAMS):
            streams, next_q, cut = sub_step(u, streams, next_q, cut)
        return tuple(streams), next_q, cut

    def work_left(state):
        streams, next_q, _ = state
        busy = next_q < n_q
        for _, _, real, _ in streams:
            busy = jnp.logical_or(busy, real == 1)
        return busy

    idle = (jnp.int32(0), jnp.int32(1), jnp.int32(0), jnp.int32(1))
    state = lax.while_loop(work_left, one_round, ((idle,) * N_STREAMS, jnp.int32(0), False))
    cut_ref[...] = jnp.full(cut_ref.shape, state[2], jnp.int32)


def _attn(q, k, v, tq, seqs_per_step, past=None, recent_only=False):
    b, sq, d = q.shape
    sk = k.shape[1]
    tq = _row_tile(sq, tq)
    tk = min(sk, max(tq, MXU_WIDTH))
    nb = seqs_per_step
    cached = 0 if past is None else past[0].shape[-1]
    assert cached % tk == 0
    q_pos0 = min(cached, tk) if recent_only else cached
    n_past = q_pos0 // tk
    last_tile = cached // tk - 1
    assert sk % tk == 0 and q_pos0 % tk == 0 and sq <= sk and (tq == tk or sq == tq)
    assert b % nb == 0
    n_col = d // LANES
    nh = HEADS_PER_STEP
    q_spec = pl.BlockSpec((nb, sq, LANES), lambda i, c: (i, 0, c))
    kv_spec = pl.BlockSpec((nb, sk, LANES), lambda i, c: (i, 0, c))
    in_specs, args = [q_spec, kv_spec, kv_spec], [q, k, v]
    if past is not None:
        layer = past[2]
        in_specs += [pl.BlockSpec((1, nb, nh, HEAD_DIM, q_pos0),
                                  lambda i, c: (layer, i, c, 0, last_tile if recent_only else 0))] * 2
        args += list(past[:2])
    ring = lambda width, dt: pltpu.VMEM((N_STREAMS, nh, tq, width), dt)
    return pl.pallas_call(
        functools.partial(_attn_body, q_pos0, n_past),
        grid=(b // nb, n_col),
        in_specs=in_specs,
        out_specs=[q_spec, pl.BlockSpec((1, 1, SUBLANES, LANES), lambda i, c: (i, c, 0, 0))],
        out_shape=[jax.ShapeDtypeStruct((b, sq, d), BF16),
                   jax.ShapeDtypeStruct((b // nb, n_col, SUBLANES, LANES), jnp.int32)],
        scratch_shapes=[pltpu.VMEM((2, tq, tk), F32), pltpu.VMEM((tk, tk), BF16),
                        ring(tk, F32), ring(tk, BF16), ring(tk, F32), ring(tk, BF16),
                        ring(1, F32), pltpu.VMEM((N_STREAMS, tq, nh * HEAD_DIM), F32)],
        compiler_params=_params(2),
        name="attn",
    )(*args)


def _merge_body(x_ref, y_ref, o_ref, sr_ref, sa_ref, wbr_ref, wba_ref, wo_ref, out_ref):
    m = (sr_ref[...].astype(F32) * _dot(y_ref[...], wbr_ref[...])
         + sa_ref[...].astype(F32) * _dot(o_ref[...], wba_ref[...]))
    out_ref[...] = x_ref[...] + _dot(m.astype(BF16), wo_ref[...])


def _merge(x, y, o, sr, sa, layer, wbr, wba, wo, tm):
    rows, d = x.shape
    tm = _row_tile(rows, tm)
    return pl.pallas_call(
        _merge_body,
        grid=(rows // tm,),
        in_specs=[_row_spec(tm, d)] * 5 + [_resident(layer, (d, d))] * 3,
        out_specs=_row_spec(tm, d),
        out_shape=jax.ShapeDtypeStruct((rows, d), F32),
        compiler_params=_params(1),
        name="merge",
    )(x, y, o, sr, sa, wbr, wba, wo)


def _block_diag(w):
    depth, n, c, _ = w.shape
    eye = jnp.eye(n, dtype=w.dtype)
    return (eye[:, None, :, None] * w[:, :, :, None, :]).reshape(depth, n * c, n * c)


def _layer(x, p_all, lw, past, h0, cbuf, k_all, v_all, layer, depth, final, tm, tt, tq, seqs):
    b, s, d = x.shape
    rows = b * s
    x = x.reshape(rows, d)
    x = _ffn(x, layer, lw["n_ffn1"], lw["f1_g"], lw["f1_u"], lw["f1_d"], tm)
    ux, gg, q, kb, vb, sr, sa, k_all, v_all = _inproj(x, lw["n_mix"], lw["w_in"], k_all, v_all,
                                                      layer, depth, b, s, tm)
    shp = (b, s, d)
    y, h_last, new_conv = _rnn(ux.reshape(shp), gg.reshape(shp), layer, lw["conv_w"], lw["conv_b"],
                               lw["wa"], lw["ba"], lw["wx"], lw["bx"], lw["lam"], h0, cbuf,
                               0 if past is None else past[0].shape[-1], tt)
    kb, vb = kb.reshape(shp), vb.reshape(shp)
    if past is not None:
        fill = ((0, 0), (0, (-s) % MXU_WIDTH), (0, 0))
        kb, vb = jnp.pad(kb, fill), jnp.pad(vb, fill)
    if past is None:
        o, _ = _attn(q.reshape(shp), kb, vb, tq, seqs)
    else:
        o, cut = _attn(q.reshape(shp), kb, vb, tq, seqs, past, recent_only=True)
        o = lax.cond(jnp.any(cut != 0),
                     lambda: _attn(q.reshape(shp), kb, vb, tq, seqs, past)[0], lambda: o)
    x = _merge(x, y.reshape(rows, d), o.reshape(rows, d), sr, sa, layer,
               lw["w_br"], lw["w_ba"], lw["w_out"], tm)
    x = _ffn(x, layer, lw["n_ffn2"], lw["f2_g"], lw["f2_u"], lw["f2_d"], tm,
             ple=(p_all.reshape(depth, rows, -1), lw["n_ple"], lw["ple_g"], lw["ple_p"], lw["final"],
                  final))
    return x.reshape(shp), k_all, v_all, h_last.reshape(b, d), new_conv


def kernel(x_prompt, x_sample, p_prompt, p_sample, cache_k, cache_v, state_h, state_conv, norm_ffn1, ffn1_w_gate, ffn1_w_up, ffn1_w_down, norm_mix, w_in, conv_w, conv_b, lru_w_a, lru_b_a, lru_w_x, lru_b_x, lru_lambda, w_branch_rnn, w_branch_attn, w_out, norm_ffn2, ffn2_w_gate, ffn2_w_up, ffn2_w_down, norm_ple, ple_w_gate, ple_w_proj, final_norm):
    depth = w_in.shape[0]
    b, s, d = x_prompt.shape
    bd, sd, _ = x_sample.shape
    keys_minor = lambda a: jnp.transpose(a, (0, 1, 3, 4, 2))
    row = lambda a: a.reshape(a.shape[0], 1, -1)
    cast = lambda a: a.astype(BF16)
    lw = dict(
        n_ffn1=row(norm_ffn1), f1_g=cast(ffn1_w_gate), f1_u=cast(ffn1_w_up), f1_d=cast(ffn1_w_down),
        n_mix=row(norm_mix), w_in=cast(w_in), conv_w=conv_w, conv_b=row(conv_b),
        wa=cast(_block_diag(lru_w_a)), ba=row(lru_b_a), wx=cast(_block_diag(lru_w_x)),
        bx=row(lru_b_x), lam=row(lru_lambda),
        w_br=cast(w_branch_rnn), w_ba=cast(w_branch_attn), w_out=cast(w_out),
        n_ffn2=row(norm_ffn2), f2_g=cast(ffn2_w_gate), f2_u=cast(ffn2_w_up), f2_d=cast(ffn2_w_down),
        n_ple=row(norm_ple), ple_g=cast(ple_w_gate), ple_p=cast(ple_w_proj),
        final=final_norm.reshape(1, 1, -1))
    xp, xs = x_prompt, x_sample
    kp = vp = ks = vs = None
    outs = [[] for _ in range(4)]
    for i in range(depth):
        final = i == depth - 1
        xp, kp, vp, hp, cp = _layer(
            xp, p_prompt, lw, None, jnp.zeros((b, 1, d), F32),
            jnp.zeros((b, CONV_W - 1, d), F32), kp, vp, i, depth, final,
            tm=512, tt=512, tq=256, seqs=2)
        xs, ks, vs, hs, cs = _layer(
            xs, p_sample, lw, (keys_minor(cache_k), keys_minor(cache_v), i), state_h[i].reshape(bd, 1, d),
            state_conv[i], ks, vs, i, depth, final, tm=bd * sd, tt=sd, tq=sd, seqs=bd)
        for lst, val in zip(outs, (hp, cp, hs, cs)):
            lst.append(val)
    hp, cp, hs, cs = (jnp.stack(lst) for lst in outs)
    seq_major = lambda a: jnp.transpose(a, (0, 1, 4, 2, 3))
    return (xp, xs, seq_major(kp), seq_major(vp), hp, cp, seq_major(ks), seq_major(vs), hs, cs)
```
